```python
import math, functools
import jax, jax.numpy as jnp
from jax import lax
import numpy as np

D_MODEL = 1024
BATCH = 4
SEQ = 4096
DEPTH = 1
DEC_BATCH = 128
DEC_SEQ = 1
PAST_LEN = 8192
PAGE_SIZE = 128

DA_HEADS = 4
DA_HEAD_DIM = 64
DA_V_DIM = 2 * DA_HEAD_DIM
DA_WIDTH = DA_HEADS * DA_V_DIM
ML_HEADS = 4
ML_HEAD_DIM = 128
ML_WIDTH = ML_HEADS * ML_HEAD_DIM
CONV_W = 4
ML_CHUNK = 64
FFN_HIDDEN = -(-8 * D_MODEL // (3 * 256)) * 256

ROPE_THETA = 10000.0
NORM_EPS = 1e-6
Q_BLOCK = 128

SPLIT_SIZES = (
    2 * DA_HEADS * DA_HEAD_DIM,
    2 * DA_HEADS * DA_HEAD_DIM,
    DA_WIDTH,
    ML_WIDTH,
    ML_WIDTH,
    ML_WIDTH,
    ML_HEADS,
    ML_HEADS,
    2 * D_MODEL,
)
SPLIT_POINTS = tuple(int(s) for s in np.cumsum(SPLIT_SIZES)[:-1])
IN_COLS = int(sum(SPLIT_SIZES))

kernel_name = 'diffattn_mlstm_gated_hybrid_step'


def rms_norm(x, g):
    xf = x.astype(jnp.float32)
    y = xf * lax.rsqrt(jnp.mean(xf * xf, axis=-1, keepdims=True) + NORM_EPS)
    return (y * g.astype(jnp.float32)).astype(x.dtype)


def rope(x, pos):
    half = x.shape[-1] // 2
    inv = ROPE_THETA ** (-jnp.arange(half, dtype=jnp.float32) / half)
    ang = pos.astype(jnp.float32)[:, None] * inv[None, :]
    cos = jnp.cos(ang)[:, None, :]
    sin = jnp.sin(ang)[:, None, :]
    xf = x.astype(jnp.float32)
    x1, x2 = xf[..., :half], xf[..., half:]
    return jnp.concatenate([x1 * cos - x2 * sin, x2 * cos + x1 * sin], axis=-1).astype(x.dtype)


def diff_attn_prompt(q, k, v, lam):
    B, T = q.shape[:2]
    nb = T // Q_BLOCK
    qb = q.reshape(B, nb, Q_BLOCK, 2 * DA_HEADS, DA_HEAD_DIM).swapaxes(0, 1)
    kpos = jnp.arange(T)
    scale = DA_HEAD_DIM ** -0.5

    def block(args):
        qi, bi = args
        s = jnp.einsum('bqhd,bkhd->bhqk', qi, k).astype(jnp.float32) * scale
        qpos = bi * Q_BLOCK + jnp.arange(Q_BLOCK)
        s = jnp.where(kpos[None, :] <= qpos[:, None], s, -jnp.inf)
        pr = jax.nn.softmax(s, axis=-1).reshape(B, DA_HEADS, 2, Q_BLOCK, T)
        a = (pr[:, :, 0] - lam * pr[:, :, 1]).astype(v.dtype)
        return jnp.einsum('bhqk,bkhe->bqhe', a, v)

    o = lax.map(block, (qb, jnp.arange(nb)))
    return o.swapaxes(0, 1).reshape(B, T, DA_HEADS, DA_V_DIM)


def diff_attn_sample(q, k, v, lam, k_past, v_past):
    T = q.shape[1]
    P = k_past.shape[1]
    scale = DA_HEAD_DIM ** -0.5
    s_past = jnp.einsum('bqhd,bkhd->bhqk', q, k_past).astype(jnp.float32)
    s_new = jnp.einsum('bqhd,bkhd->bhqk', q, k).astype(jnp.float32)
    causal = jnp.tril(jnp.ones((T, T), dtype=bool))
    s_new = jnp.where(causal, s_new, -jnp.inf)
    s = jnp.concatenate([s_past, s_new], axis=-1) * scale
    pr = jax.nn.softmax(s, axis=-1).reshape(q.shape[0], DA_HEADS, 2, T, P + T)
    a = (pr[:, :, 0] - lam * pr[:, :, 1]).astype(v.dtype)
    return (jnp.einsum('bhqk,bkhe->bqhe', a[..., :P], v_past)
            + jnp.einsum('bhqk,bkhe->bqhe', a[..., P:], v))


def causal_conv(u_ext, w, b, T):
    out = b
    for j in range(CONV_W):
        out = out + u_ext[:, j:j + T] * w[j]
    return out


def mlstm_chunkwise(q, k, v, ig, lf, C0, n0, m0):
    B, T, H, D = q.shape
    L = ML_CHUNK if T % ML_CHUNK == 0 else T
    nc = T // L
    causal = jnp.tril(jnp.ones((L, L), dtype=bool))

    def to_chunks(a):
        return a.reshape((B, nc, L) + a.shape[2:]).swapaxes(0, 1)

    def step(carry, xs):
        C, n, m = carry
        qc, kc, vc, igc, lfc = xs
        b = jnp.cumsum(lfc, axis=1).swapaxes(1, 2)
        igh = igc.swapaxes(1, 2)
        dmat = b[..., :, None] - b[..., None, :] + igh[..., None, :]
        dmat = jnp.where(causal, dmat, -jnp.inf)
        inter = b + m[..., None]
        m_t = jnp.maximum(inter, jnp.max(dmat, axis=-1))
        w_intra = jnp.exp(dmat - m_t[..., None])
        w_inter = jnp.exp(inter - m_t)
        s = jnp.einsum('bthd,bshd->bhts', qc, kc) * w_intra
        num = (w_inter[..., None] * jnp.einsum('bthd,bhde->bhte', qc, C)
               + jnp.einsum('bhts,bshe->bhte', s, vc))
        den = w_inter * jnp.einsum('bthd,bhd->bht', qc, n) + jnp.sum(s, axis=-1)
        h = num / jnp.maximum(jnp.abs(den), jnp.exp(-m_t))[..., None]
        m_new = m_t[..., -1]
        w_end = jnp.exp(b[..., -1:] - b + igh - m_new[..., None])
        decay = jnp.exp(b[..., -1] + m - m_new)
        C_new = decay[..., None, None] * C + jnp.einsum('bhs,bshd,bshe->bhde', w_end, kc, vc)
        n_new = decay[..., None] * n + jnp.einsum('bhs,bshd->bhd', w_end, kc)
        return (C_new, n_new, m_new), h.swapaxes(1, 2)

    xs = (to_chunks(q), to_chunks(k), to_chunks(v), to_chunks(ig), to_chunks(lf))
    (C, n, m), hs = lax.scan(step, (C0, n0, m0), xs)
    return hs.swapaxes(0, 1).reshape(B, T, H, D), C, n, m


def hybrid_layer(x, pos, conv_buf, C0, n0, m0, attend, p, lam, lam_init):
    B, T = x.shape[:2]
    h = rms_norm(x, p['norm1_g'])
    z = h @ p['w_in']
    qa, ka, va, u, mv, mo, ig, fg, gates = jnp.split(z, SPLIT_POINTS, axis=-1)

    qa = rope(rms_norm(qa.reshape(B, T, 2 * DA_HEADS, DA_HEAD_DIM), p['qnorm_g']), pos)
    ka = rope(rms_norm(ka.reshape(B, T, 2 * DA_HEADS, DA_HEAD_DIM), p['knorm_g']), pos)
    va = va.reshape(B, T, DA_HEADS, DA_V_DIM)
    oa = attend(qa, ka, va, lam)
    oa = rms_norm(oa, p['subln_g']) * (1.0 - lam_init)
    ya = oa.reshape(B, T, DA_WIDTH) @ p['w_a_out']

    u_ext = jnp.concatenate([conv_buf.astype(u.dtype), u], axis=1)
    c = jax.nn.silu(causal_conv(u_ext, p['conv_w'], p['conv_b'], T))
    c = c.reshape(B, T, ML_HEADS, ML_HEAD_DIM)
    mq = jnp.einsum('bthd,hde->bthe', c, p['w_mq']).astype(jnp.float32)
    mk = jnp.einsum('bthd,hde->bthe', c, p['w_mk']).astype(jnp.float32) * (ML_HEAD_DIM ** -0.5)
    mvh = mv.reshape(B, T, ML_HEADS, ML_HEAD_DIM).astype(jnp.float32)
    log_i = ig.astype(jnp.float32) + p['b_igate'].astype(jnp.float32)
    log_f = jax.nn.log_sigmoid(fg.astype(jnp.float32) + p['b_fgate'].astype(jnp.float32))
    hm, C, n, m = mlstm_chunkwise(mq, mk, mvh, log_i, log_f, C0.astype(jnp.float32),
                                  n0.astype(jnp.float32), m0.astype(jnp.float32))
    hm = rms_norm(hm, p['mnorm_g']).astype(x.dtype) * jax.nn.sigmoid(mo).reshape(B, T, ML_HEADS, ML_HEAD_DIM)
    yb = hm.reshape(B, T, ML_WIDTH) @ p['w_b_out']

    g = jax.nn.sigmoid(gates + p['b_merge'])
    x = x + (g[..., :D_MODEL] * ya + g[..., D_MODEL:] * yb) @ p['w_o']

    h2 = rms_norm(x, p['norm2_g'])
    x = x + (jax.nn.silu(h2 @ p['w_ffn_gate']) * (h2 @ p['w_ffn_up'])) @ p['w_ffn_down']
    return x, ka, va, C, n, m, u_ext[:, -(CONV_W - 1):]


def setup_inputs(seed: int = 0) -> dict:
    key = jax.random.key(seed)
    ks = jax.random.split(key, 40)
    f32 = jnp.float32

    def nrm(k, shape, s):
        return jax.random.normal(k, shape, f32) * s

    n_pages = PAST_LEN // PAGE_SIZE
    n_used = DEC_BATCH * n_pages
    n_phys = n_used + max(1, n_used // 4)
    page_table = jax.random.permutation(ks[0], n_phys)[:n_used].reshape(DEC_BATCH, n_pages).astype(jnp.int32)

    return {
        'x_prompt': nrm(ks[1], (BATCH, SEQ, D_MODEL), 1.0),
        'x_sample': nrm(ks[2], (DEC_BATCH, DEC_SEQ, D_MODEL), 1.0),
        'cache_k': nrm(ks[3], (DEPTH, n_phys, PAGE_SIZE, 2 * DA_HEADS, DA_HEAD_DIM), 1.0),
        'cache_v': nrm(ks[4], (DEPTH, n_phys, PAGE_SIZE, DA_HEADS, DA_V_DIM), 1.0),
        'page_table': page_table,
        'state_C': nrm(ks[5], (DEPTH, DEC_BATCH, ML_HEADS, ML_HEAD_DIM, ML_HEAD_DIM), ML_HEAD_DIM ** -0.5),
        'state_n': nrm(ks[6], (DEPTH, DEC_BATCH, ML_HEADS, ML_HEAD_DIM), ML_HEAD_DIM ** -0.5),
        'state_m': nrm(ks[7], (DEPTH, DEC_BATCH, ML_HEADS), 1.0),
        'state_conv': nrm(ks[8], (DEPTH, DEC_BATCH, CONV_W - 1, ML_WIDTH), 1.0),
        'norm1_g': 1.0 + nrm(ks[9], (DEPTH, D_MODEL), 0.02),
        'w_in': nrm(ks[10], (DEPTH, D_MODEL, IN_COLS), D_MODEL ** -0.5),
        'qnorm_g': 1.0 + nrm(ks[11], (DEPTH, DA_HEAD_DIM), 0.02),
        'knorm_g': 1.0 + nrm(ks[12], (DEPTH, DA_HEAD_DIM), 0.02),
        'lambda_q1': nrm(ks[13], (DEPTH, DA_HEAD_DIM), 0.1),
        'lambda_k1': nrm(ks[14], (DEPTH, DA_HEAD_DIM), 0.1),
        'lambda_q2': nrm(ks[15], (DEPTH, DA_HEAD_DIM), 0.1),
        'lambda_k2': nrm(ks[16], (DEPTH, DA_HEAD_DIM), 0.1),
        'subln_g': 1.0 + nrm(ks[17], (DEPTH, DA_V_DIM), 0.02),
        'w_a_out': nrm(ks[18], (DEPTH, DA_WIDTH, D_MODEL), DA_WIDTH ** -0.5),
        'conv_w': nrm(ks[19], (DEPTH, CONV_W, ML_WIDTH), CONV_W ** -0.5),
        'conv_b': nrm(ks[20], (DEPTH, ML_WIDTH), 0.02),
        'w_mq': nrm(ks[21], (DEPTH, ML_HEADS, ML_HEAD_DIM, ML_HEAD_DIM), ML_HEAD_DIM ** -0.5),
        'w_mk': nrm(ks[22], (DEPTH, ML_HEADS, ML_HEAD_DIM, ML_HEAD_DIM), ML_HEAD_DIM ** -0.5),
        'b_igate': nrm(ks[23], (DEPTH, ML_HEADS), 0.1),
        'b_fgate': 3.0 + nrm(ks[24], (DEPTH, ML_HEADS), 0.5),
        'mnorm_g': 1.0 + nrm(ks[25], (DEPTH, ML_HEAD_DIM), 0.02),
        'w_b_out': nrm(ks[26], (DEPTH, ML_WIDTH, D_MODEL), ML_WIDTH ** -0.5),
        'b_merge': nrm(ks[27], (DEPTH, 2 * D_MODEL), 0.02),
        'w_o': nrm(ks[28], (DEPTH, D_MODEL, D_MODEL), D_MODEL ** -0.5),
        'norm2_g': 1.0 + nrm(ks[29], (DEPTH, D_MODEL), 0.02),
        'w_ffn_gate': nrm(ks[30], (DEPTH, D_MODEL, FFN_HIDDEN), D_MODEL ** -0.5),
        'w_ffn_up': nrm(ks[31], (DEPTH, D_MODEL, FFN_HIDDEN), D_MODEL ** -0.5),
        'w_ffn_down': nrm(ks[32], (DEPTH, FFN_HIDDEN, D_MODEL), FFN_HIDDEN ** -0.5),
    }


def reference(x_prompt, x_sample, cache_k, cache_v, page_table, state_C, state_n, state_m, state_conv,
              norm1_g, w_in, qnorm_g, knorm_g, lambda_q1, lambda_k1, lambda_q2, lambda_k2, subln_g, w_a_out,
              conv_w, conv_b, w_mq, w_mk, b_igate, b_fgate, mnorm_g, w_b_out, b_merge, w_o, norm2_g,
              w_ffn_gate, w_ffn_up, w_ffn_down):
    Bp, Tp = x_prompt.shape[:2]
    Bs, Ts = x_sample.shape[:2]
    past_len = page_table.shape[1] * cache_k.shape[2]
    pos_p = jnp.arange(Tp)
    pos_s = past_len + jnp.arange(Ts)

    yp, ys = x_prompt, x_sample
    kp_l, vp_l, Cp_l, np_l, mp_l, cp_l = [], [], [], [], [], []
    ks_l, vs_l, Cs_l, ns_l, ms_l, cs_l = [], [], [], [], [], []
    for l in range(DEPTH):
        p = {
            'norm1_g': norm1_g[l], 'w_in': w_in[l], 'qnorm_g': qnorm_g[l], 'knorm_g': knorm_g[l],
            'subln_g': subln_g[l], 'w_a_out': w_a_out[l], 'conv_w': conv_w[l], 'conv_b': conv_b[l],
            'w_mq': w_mq[l], 'w_mk': w_mk[l], 'b_igate': b_igate[l], 'b_fgate': b_fgate[l],
            'mnorm_g': mnorm_g[l], 'w_b_out': w_b_out[l], 'b_merge': b_merge[l], 'w_o': w_o[l],
            'norm2_g': norm2_g[l], 'w_ffn_gate': w_ffn_gate[l], 'w_ffn_up': w_ffn_up[l],
            'w_ffn_down': w_ffn_down[l],
        }
        lam_init = 0.8 - 0.6 * math.exp(-0.3 * l)
        lam = (jnp.exp(jnp.sum(lambda_q1[l].astype(jnp.float32) * lambda_k1[l].astype(jnp.float32)))
               - jnp.exp(jnp.sum(lambda_q2[l].astype(jnp.float32) * lambda_k2[l].astype(jnp.float32)))
               + lam_init)

        conv0 = jnp.zeros((Bp, CONV_W - 1, ML_WIDTH), x_prompt.dtype)
        C0 = jnp.zeros((Bp, ML_HEADS, ML_HEAD_DIM, ML_HEAD_DIM), jnp.float32)
        n0 = jnp.zeros((Bp, ML_HEADS, ML_HEAD_DIM), jnp.float32)
        m0 = jnp.zeros((Bp, ML_HEADS), jnp.float32)
        yp, kp, vp, Cp, np_, mp, cp = hybrid_layer(yp, pos_p, conv0, C0, n0, m0, diff_attn_prompt, p, lam, lam_init)

        k_past = cache_k[l][page_table].reshape(Bs, past_len, 2 * DA_HEADS, DA_HEAD_DIM)
        v_past = cache_v[l][page_table].reshape(Bs, past_len, DA_HEADS, DA_V_DIM)
        attend_s = functools.partial(diff_attn_sample, k_past=k_past, v_past=v_past)
        ys, ksn, vsn, Cs, ns, ms, cs = hybrid_layer(ys, pos_s, state_conv[l], state_C[l], state_n[l],
                                                    state_m[l], attend_s, p, lam, lam_init)

        kp_l.append(kp); vp_l.append(vp); Cp_l.append(Cp); np_l.append(np_); mp_l.append(mp); cp_l.append(cp)
        ks_l.append(ksn); vs_l.append(vsn); Cs_l.append(Cs); ns_l.append(ns); ms_l.append(ms); cs_l.append(cs)

    return (yp, ys,
            jnp.stack(kp_l), jnp.stack(vp_l), jnp.stack(Cp_l), jnp.stack(np_l), jnp.stack(mp_l), jnp.stack(cp_l),
            jnp.stack(ks_l), jnp.stack(vs_l), jnp.stack(Cs_l), jnp.stack(ns_l), jnp.stack(ms_l), jnp.stack(cs_l))
```

```python
import functools
import math

import jax
import jax.numpy as jnp
import numpy as np
from jax import lax
from jax.experimental import pallas as pl
from jax.experimental.pallas import tpu as pltpu

F32 = jnp.float32
BF16 = jnp.bfloat16

DA_HEADS = 4
DA_HEAD_DIM = 64
DA_V_DIM = 2 * DA_HEAD_DIM
ML_HEADS = 4
ML_HEAD_DIM = 128
CONV_W = 4
ROPE_THETA = 10000.0
NORM_EPS = 1e-6

LANES = 128
SUBLANES = 8
VMEM_LIMIT = 56 * 1024 * 1024

_HIGHEST = lax.Precision.HIGHEST


def _cparams(sem):
    return pltpu.CompilerParams(dimension_semantics=sem, vmem_limit_bytes=VMEM_LIMIT)


def _dot(a, b):
    return jnp.dot(a, b, preferred_element_type=F32)


def _dot_nt(a, b):
    return lax.dot_general(a, b, (((1,), (1,)), ((), ())), preferred_element_type=F32)


def _dot_tn(a, b):
    return lax.dot_general(a, b, (((0,), (0,)), ((), ())), preferred_element_type=F32)


def _log_sigmoid(x):
    return jnp.minimum(x, 0.0) - jnp.log(1.0 + jnp.exp(-jnp.abs(x)))


def _in_proj_kernel(x_ref, g1_ref, w_ref, qg_ref, kg_ref, cos_ref, sin_ref, gm_ref,
                    q_ref, k_ref, kb_ref, v_ref, vb_ref, u_ref, mv_ref, mo_ref, zif_ref):
    x = x_ref[...]
    ms = jnp.mean(x * x, axis=-1, keepdims=True)
    h = (x * lax.rsqrt(ms + NORM_EPS) * g1_ref[...]).astype(BF16)
    z = _dot(h, w_ref[...])
    width = DA_HEADS * DA_V_DIM
    cos = jnp.concatenate([cos_ref[...]] * (width // LANES), axis=1)
    sin = jnp.concatenate([sin_ref[...]] * (width // LANES), axis=1)
    lane = lax.broadcasted_iota(jnp.int32, (x.shape[0], width), 1)
    first_half = (lane % DA_HEAD_DIM) < (DA_HEAD_DIM // 2)

    def qk_norm_rope(t, g):
        ms_g = _dot((t * t).astype(BF16), gm_ref[...])
        y = t * lax.rsqrt(ms_g + NORM_EPS) * g
        partner = jnp.where(first_half,
                            pltpu.roll(y, width - DA_HEAD_DIM // 2, 1),
                            pltpu.roll(y, DA_HEAD_DIM // 2, 1))
        return y * cos + partner * sin

    q = qk_norm_rope(z[:, 0:width], qg_ref[...])
    k = qk_norm_rope(z[:, width:2 * width], kg_ref[...])
    q_ref[...] = (q * (DA_HEAD_DIM ** -0.5)).astype(BF16)
    k_ref[...] = k
    kb_ref[...] = k.astype(BF16)
    v = z[:, 2 * width:3 * width]
    v_ref[...] = v
    vb_ref[...] = v.astype(BF16)
    u_ref[...] = z[:, 3 * width:4 * width]
    mv_ref[...] = z[:, 4 * width:5 * width].astype(BF16)
    mo_ref[...] = z[:, 5 * width:6 * width]
    zif_ref[...] = z[:, 6 * width:6 * width + LANES]


def _in_proj(x2, g1, w, qg, kg, cos, sin, gm, tm):
    M, D = x2.shape
    W = 512
    tab_blocks = cos.shape[0] // tm
    row = lambda i: (i, 0)
    tab = lambda i: (i % tab_blocks, 0)
    fixed = lambda i: (0, 0)
    outs = [
        jax.ShapeDtypeStruct((M, W), BF16),
        jax.ShapeDtypeStruct((M, W), F32),
        jax.ShapeDtypeStruct((M, W), BF16),
        jax.ShapeDtypeStruct((M, W), F32),
        jax.ShapeDtypeStruct((M, W), BF16),
        jax.ShapeDtypeStruct((M, W), F32),
        jax.ShapeDtypeStruct((M, W), BF16),
        jax.ShapeDtypeStruct((M, W), F32),
        jax.ShapeDtypeStruct((M, LANES), F32),
    ]
    return pl.pallas_call(
        _in_proj_kernel,
        grid=(M // tm,),
        in_specs=[
            pl.BlockSpec((tm, D), row),
            pl.BlockSpec((1, D), fixed),
            pl.BlockSpec(w.shape, fixed),
            pl.BlockSpec((1, W), fixed),
            pl.BlockSpec((1, W), fixed),
            pl.BlockSpec((tm, LANES), tab),
            pl.BlockSpec((tm, LANES), tab),
            pl.BlockSpec((W, W), fixed),
        ],
        out_specs=[pl.BlockSpec((tm, o.shape[1]), row) for o in outs],
        out_shape=outs,
        compiler_params=_cparams(("parallel",)),
        name="in_proj",
    )(x2, g1, w, qg, kg, cos, sin, gm)


def _lambda_value(lams, lam_init):
    a = jnp.sum(lams[0:1, :] * lams[1:2, :], axis=1, keepdims=True)
    b = jnp.sum(lams[2:3, :] * lams[3:4, :], axis=1, keepdims=True)
    return jnp.exp(a) - jnp.exp(b) + lam_init


def _attn_prompt_kernel(qi_tab, ki_tab, lams_ref, q_ref, k_ref, v_ref, sg_ref, o_ref,
                        qs, m_s, l_s, acc_s, *, tq, lam_init):
    step = pl.program_id(2)
    qi = qi_tab[step]
    ki = ki_tab[step]

    @pl.when(ki == 0)
    def _():
        q = q_ref[...].astype(F32)
        lane = lax.broadcasted_iota(jnp.int32, q.shape, 1)
        qs[0:tq, :] = jnp.where(lane < DA_HEAD_DIM, q, 0.0).astype(BF16)
        qs[tq:2 * tq, :] = jnp.where(lane >= DA_HEAD_DIM, q, 0.0).astype(BF16)
        m_s[...] = jnp.full(m_s.shape, -jnp.inf, F32)
        l_s[...] = jnp.zeros(l_s.shape, F32)
        acc_s[...] = jnp.zeros(acc_s.shape, F32)

    def update(masked):
        s = _dot_nt(qs[...], k_ref[...])
        if masked:
            row = lax.broadcasted_iota(jnp.int32, s.shape, 0)
            row = jnp.where(row >= tq, row - tq, row)
            col = lax.broadcasted_iota(jnp.int32, s.shape, 1)
            s = jnp.where(col <= row, s, -jnp.inf)
        m_prev = m_s[...]
        m_new = jnp.maximum(m_prev, jnp.max(s, axis=1, keepdims=True))
        alpha = jnp.exp(m_prev - m_new)
        p = jnp.exp(s - m_new)
        l_s[...] = alpha * l_s[...] + jnp.sum(p, axis=1, keepdims=True)
        acc_s[...] = alpha * acc_s[...] + _dot(p.astype(BF16), v_ref[...])
        m_s[...] = m_new

    @pl.when(ki < qi)
    def _():
        update(False)

    @pl.when(ki == qi)
    def _():
        update(True)
        o = acc_s[...] / l_s[...]
        lam = _lambda_value(lams_ref[...], lam_init)
        d = o[0:tq, :] - lam * o[tq:2 * tq, :]
        ms = jnp.mean(d * d, axis=-1, keepdims=True)
        y = d * lax.rsqrt(ms + NORM_EPS) * sg_ref[...] * (1.0 - lam_init)
        o_ref[...] = y.astype(BF16)


def _attn_prompt(q, k, v, lams, sg, lam_init, tq):
    B, T, W = q.shape
    nq = T // tq
    qi_tab = np.concatenate([np.full(i + 1, i, np.int32) for i in range(nq)])
    ki_tab = np.concatenate([np.arange(i + 1, dtype=np.int32) for i in range(nq)])
    nsteps = int(qi_tab.shape[0])
    grid_spec = pltpu.PrefetchScalarGridSpec(
        num_scalar_prefetch=2,
        grid=(B, DA_HEADS, nsteps),
        in_specs=[
            pl.BlockSpec((4, DA_HEAD_DIM), lambda b, h, s, qt, kt: (0, 0)),
            pl.BlockSpec((None, tq, LANES), lambda b, h, s, qt, kt: (b, qt[s], h)),
            pl.BlockSpec((None, tq, LANES), lambda b, h, s, qt, kt: (b, kt[s], h)),
            pl.BlockSpec((None, tq, LANES), lambda b, h, s, qt, kt: (b, kt[s], h)),
            pl.BlockSpec((1, LANES), lambda b, h, s, qt, kt: (0, 0)),
        ],
        out_specs=pl.BlockSpec((None, tq, LANES), lambda b, h, s, qt, kt: (b, qt[s], h)),
        scratch_shapes=[
            pltpu.VMEM((2 * tq, LANES), BF16),
            pltpu.VMEM((2 * tq, 1), F32),
            pltpu.VMEM((2 * tq, 1), F32),
            pltpu.VMEM((2 * tq, LANES), F32),
        ],
    )
    return pl.pallas_call(
        functools.partial(_attn_prompt_kernel, tq=tq, lam_init=lam_init),
        grid_spec=grid_spec,
        out_shape=jax.ShapeDtypeStruct((B, T, W), BF16),
        compiler_params=_cparams(("parallel", "parallel", "arbitrary")),
        name="attn_prompt",
    )(jnp.asarray(qi_tab), jnp.asarray(ki_tab), lams, q, k, v, sg)


def _attn_sample_kernel(pt_ref, lams_ref, q_ref, kn_ref, vn_ref, sg_ref, *rest, G, NP, lam_init):
    k_refs = rest[0:G]
    v_refs = rest[G:2 * G]
    o_ref = rest[2 * G]
    qbd, s_t, p_t, dexp, acc = rest[2 * G + 1:]
    NA = NP // G
    step = pl.program_id(1)
    page_rows = k_refs[0].shape[0]
    past = NP * page_rows
    W = qbd.shape[1]

    @pl.when(step == 0)
    def _():
        q = jnp.broadcast_to(q_ref[...].astype(F32), qbd.shape)
        r = lax.broadcasted_iota(jnp.int32, qbd.shape, 0)
        c = lax.broadcasted_iota(jnp.int32, qbd.shape, 1)
        keep = (c // DA_HEAD_DIM) == jnp.where(r < 16, r % 8, -1)
        qbd[...] = jnp.where(keep, q, 0.0).astype(BF16)
        acc[...] = jnp.zeros(acc.shape, F32)

    @pl.when(step < NA)
    def _():
        for i in range(G):
            kb = k_refs[i][...].astype(BF16)
            start = pl.multiple_of((step * G + i) * page_rows, page_rows)
            s_t[pl.ds(start, page_rows), :] = _dot_nt(kb, qbd[...])

    @pl.when(step == NA - 1)
    def _():
        kn = jnp.broadcast_to(kn_ref[...], (16, W)).astype(BF16)
        sn = _dot_nt(kn, qbd[...])
        r16 = lax.broadcasted_iota(jnp.int32, sn.shape, 0)
        s_t[past:past + 16, :] = jnp.where(r16 == 0, sn, -jnp.inf)
        s_all = s_t[...]
        m = jnp.max(s_all, axis=0, keepdims=True)
        p = jnp.exp(s_all - m)
        l = jnp.sum(p, axis=0, keepdims=True)
        p_t[...] = p.astype(BF16)
        lam = _lambda_value(lams_ref[...], lam_init)
        l_rows = jnp.transpose(jnp.broadcast_to(l, (LANES, LANES)))
        r = lax.broadcasted_iota(jnp.int32, (LANES, LANES), 0)
        coef = jnp.where((r % 2) == 0, 1.0, -lam) / l_rows
        hi = coef.astype(BF16).astype(F32)
        part = jnp.where(r < 8, hi, jnp.where(r < 16, coef - hi, 0.0))
        part = jnp.concatenate([part] * (W // LANES), axis=1)
        rw = lax.broadcasted_iota(jnp.int32, (LANES, W), 0)
        cw = lax.broadcasted_iota(jnp.int32, (LANES, W), 1)
        dexp[...] = jnp.where((cw // DA_V_DIM) == ((rw % 8) // 2), part, 0.0).astype(BF16)

    @pl.when(step >= NA)
    def _():
        total = acc[...]
        for i in range(G):
            start = pl.multiple_of(((step - NA) * G + i) * page_rows, page_rows)
            w = _dot(p_t[pl.ds(start, page_rows), :], dexp[...])
            wv = w * v_refs[i][...]
            total = total + jnp.sum(wv.reshape(page_rows // SUBLANES, SUBLANES, W), axis=0)
        acc[...] = total

    @pl.when(step == 2 * NA - 1)
    def _():
        wn = _dot(p_t[past:past + 16, :], dexp[...])
        o = (jnp.sum(acc[...], axis=0, keepdims=True)
             + jnp.sum(wn * vn_ref[...], axis=0, keepdims=True))
        parts = []
        for h in range(DA_HEADS):
            oh = o[:, h * DA_V_DIM:(h + 1) * DA_V_DIM]
            ms = jnp.mean(oh * oh, axis=-1, keepdims=True)
            parts.append(oh * lax.rsqrt(ms + NORM_EPS) * sg_ref[...] * (1.0 - lam_init))
        o_ref[...] = jnp.concatenate(parts, axis=1).astype(o_ref.dtype)


def _attn_sample(q, k_new, v_new, cache_k, cache_v, page_table, lams, sg, lam_init, G):
    Bd, W = q.shape
    NP = page_table.shape[1]
    page_rows = cache_k.shape[1]
    NA = NP // G
    pt = page_table.reshape(-1)
    past = NP * page_rows

    def k_map(i):
        return lambda b, s, pt: (pt[b * NP + jnp.minimum(s, NA - 1) * G + i], 0, 0)

    def v_map(i):
        return lambda b, s, pt: (pt[b * NP + jnp.maximum(s - NA, 0) * G + i], 0, 0)

    per_seq = lambda b, s, pt: (b, 0, 0)
    fixed = lambda b, s, pt: (0, 0)
    grid_spec = pltpu.PrefetchScalarGridSpec(
        num_scalar_prefetch=1,
        grid=(Bd, 2 * NA),
        in_specs=[
            pl.BlockSpec((4, DA_HEAD_DIM), fixed),
            pl.BlockSpec((None, 1, W), per_seq),
            pl.BlockSpec((None, 1, W), per_seq),
            pl.BlockSpec((None, 1, W), per_seq),
            pl.BlockSpec((1, LANES), fixed),
        ] + [pl.BlockSpec((None, page_rows, W), k_map(i)) for i in range(G)]
          + [pl.BlockSpec((None, page_rows, W), v_map(i)) for i in range(G)],
        out_specs=pl.BlockSpec((None, 1, W), per_seq),
        scratch_shapes=[
            pltpu.VMEM((LANES, W), BF16),
            pltpu.VMEM((past + 16, LANES), F32),
            pltpu.VMEM((past + 16, LANES), BF16),
            pltpu.VMEM((LANES, W), BF16),
            pltpu.VMEM((SUBLANES, W), F32),
        ],
    )
    out = pl.pallas_call(
        functools.partial(_attn_sample_kernel, G=G, NP=NP, lam_init=lam_init),
        grid_spec=grid_spec,
        out_shape=jax.ShapeDtypeStruct((Bd, 1, W), BF16),
        compiler_params=_cparams(("parallel", "arbitrary")),
        name="attn_sample",
    )(pt, lams, q.reshape(Bd, 1, W), k_new.reshape(Bd, 1, W), v_new.reshape(Bd, 1, W), sg,
      *([cache_k] * G), *([cache_v] * G))
    return out.reshape(Bd, W)


def _gate_activations(zif, bif):
    g = zif + bif
    lane = lax.broadcasted_iota(jnp.int32, g.shape, 1)
    return jnp.where(lane < ML_HEADS, g, _log_sigmoid(g))


def _mlstm_prompt_kernel(u_ref, mv_ref, mo_ref, zif_ref, cw_ref, cb_ref, wqk_ref, bif_ref, mg_ref,
                         hm_ref, C_ref, n_ref, m_ref, ext, C_s, n_s, m_s, *, L):
    chunk = pl.program_id(1)
    HD = ML_HEAD_DIM

    @pl.when(chunk == 0)
    def _():
        ext[0:SUBLANES, :] = jnp.zeros((SUBLANES, ext.shape[1]), F32)
        C_s[...] = jnp.zeros(C_s.shape, F32)
        n_s[...] = jnp.zeros(n_s.shape, F32)
        m_s[...] = jnp.zeros(m_s.shape, F32)

    u = u_ref[...]
    ext[SUBLANES:SUBLANES + L, :] = u
    conv = cb_ref[...]
    for j in range(CONV_W):
        off = SUBLANES - (CONV_W - 1) + j
        conv = conv + ext[off:off + L, :] * cw_ref[j:j + 1, :]
    ext[0:SUBLANES, :] = u[L - SUBLANES:L, :]
    c = (conv * jax.nn.sigmoid(conv)).astype(BF16)

    gact = _gate_activations(zif_ref[...], bif_ref[...])
    gact_t = jnp.transpose(gact)
    row = lax.broadcasted_iota(jnp.int32, (L, L), 0)
    col = lax.broadcasted_iota(jnp.int32, (L, L), 1)
    causal = col <= row
    tri = causal.astype(F32)
    cum_col = jnp.dot(tri, gact, precision=_HIGHEST, preferred_element_type=F32)
    cum_row = lax.dot_general(gact_t[0:SUBLANES, :], tri, (((1,), (1,)), ((), ())),
                              precision=_HIGHEST, preferred_element_type=F32)

    for h in range(ML_HEADS):
        sl = slice(h * HD, (h + 1) * HD)
        b_col = cum_col[:, ML_HEADS + h:ML_HEADS + h + 1]
        b_row = cum_row[ML_HEADS + h:ML_HEADS + h + 1, :]
        li_col = gact[:, h:h + 1]
        li_row = gact_t[h:h + 1, :]
        m_prev = m_s[h:h + 1, 0:1]
        dmat = jnp.where(causal, b_col - b_row + li_row, -jnp.inf)
        inter = b_col + m_prev
        m_t = jnp.maximum(inter, jnp.max(dmat, axis=1, keepdims=True))
        w_intra = jnp.exp(dmat - m_t)
        w_inter = jnp.exp(inter - m_t)
        qk = _dot(c[:, sl], wqk_ref[h])
        q = qk[:, 0:HD]
        k = qk[:, HD:2 * HD] * (HD ** -0.5)
        qb = q.astype(BF16)
        v = mv_ref[:, sl]
        s = _dot_nt(qb, k.astype(BF16)) * w_intra
        C = C_s[h]
        n_row = n_s[h:h + 1, :]
        num = w_inter * _dot(qb, C.astype(BF16)) + _dot(s.astype(BF16), v)
        den = w_inter * jnp.sum(q * n_row, axis=1, keepdims=True) + jnp.sum(s, axis=1, keepdims=True)
        hval = num / jnp.maximum(jnp.abs(den), jnp.exp(-m_t))
        m_new = m_t[L - 1:L, :]
        b_last = b_col[L - 1:L, :]
        w_end = jnp.exp(b_last - b_col + li_col - m_new)
        decay = jnp.exp(b_last + m_prev - m_new)
        kw = k * w_end
        C_s[h] = decay * C + _dot_tn(kw.astype(BF16), v)
        n_s[h:h + 1, :] = decay * n_row + jnp.sum(kw, axis=0, keepdims=True)
        m_s[h:h + 1, :] = jnp.broadcast_to(m_new, (1, m_s.shape[1]))
        ms = jnp.mean(hval * hval, axis=-1, keepdims=True)
        y = hval * lax.rsqrt(ms + NORM_EPS) * mg_ref[...]
        hm_ref[:, sl] = (y * jax.nn.sigmoid(mo_ref[:, sl])).astype(BF16)

    @pl.when(chunk == pl.num_programs(1) - 1)
    def _():
        C_ref[...] = C_s[...]
        n_ref[...] = n_s[...]
        m_ref[...] = m_s[...]


def _mlstm_prompt(u, mv, mo, zif, cw, cb, wqk, bif, mg, L):
    B, T, W = u.shape
    H, HD = ML_HEADS, ML_HEAD_DIM
    tok = lambda b, c: (b, c, 0)
    fixed2 = lambda b, c: (0, 0)
    outs = [
        jax.ShapeDtypeStruct((B, T, W), BF16),
        jax.ShapeDtypeStruct((B, H, HD, HD), F32),
        jax.ShapeDtypeStruct((B, SUBLANES, HD), F32),
        jax.ShapeDtypeStruct((B, SUBLANES, LANES), F32),
    ]
    return pl.pallas_call(
        functools.partial(_mlstm_prompt_kernel, L=L),
        grid=(B, T // L),
        in_specs=[
            pl.BlockSpec((None, L, W), tok),
            pl.BlockSpec((None, L, W), tok),
            pl.BlockSpec((None, L, W), tok),
            pl.BlockSpec((None, L, LANES), tok),
            pl.BlockSpec((CONV_W, W), fixed2),
            pl.BlockSpec((1, W), fixed2),
            pl.BlockSpec((H, HD, 2 * HD), lambda b, c: (0, 0, 0)),
            pl.BlockSpec((1, LANES), fixed2),
            pl.BlockSpec((1, HD), fixed2),
        ],
        out_specs=[
            pl.BlockSpec((None, L, W), tok),
            pl.BlockSpec((None, H, HD, HD), lambda b, c: (b, 0, 0, 0)),
            pl.BlockSpec((None, SUBLANES, HD), lambda b, c: (b, 0, 0)),
            pl.BlockSpec((None, SUBLANES, LANES), lambda b, c: (b, 0, 0)),
        ],
        out_shape=outs,
        scratch_shapes=[
            pltpu.VMEM((L + SUBLANES, W), F32),
            pltpu.VMEM((H, HD, HD), F32),
            pltpu.VMEM((SUBLANES, HD), F32),
            pltpu.VMEM((SUBLANES, LANES), F32),
        ],
        compiler_params=_cparams(("parallel", "arbitrary")),
        name="mlstm_prompt",
    )(u, mv, mo, zif, cw, cb, wqk, bif, mg)


def _mlstm_sample_kernel(u_ref, c0_ref, c1_ref, c2_ref, mv_ref, mo_ref, zif_ref, cw_ref, cb_ref, wqk_ref,
                         bif_ref, mg_ref, C_in, n_in, m_in, hm_ref, C_out, n_out, m_out, *, TB):
    HD = ML_HEAD_DIM
    conv = (cb_ref[...] + c0_ref[...] * cw_ref[0:1, :] + c1_ref[...] * cw_ref[1:2, :]
            + c2_ref[...] * cw_ref[2:3, :] + u_ref[...] * cw_ref[3:4, :])
    c = (conv * jax.nn.sigmoid(conv)).astype(BF16)
    gact = _gate_activations(zif_ref[...], bif_ref[...])
    m0_all = m_in[...]
    rows = lax.broadcasted_iota(jnp.int32, (TB, HD), 0)
    m_new_all = jnp.zeros((TB, LANES), F32)
    lane = lax.broadcasted_iota(jnp.int32, (TB, LANES), 1)
    for h in range(ML_HEADS):
        sl = slice(h * HD, (h + 1) * HD)
        li = gact[:, h:h + 1]
        lf = gact[:, ML_HEADS + h:ML_HEADS + h + 1]
        m0 = m0_all[:, h:h + 1]
        inter = lf + m0
        m_t = jnp.maximum(inter, li)
        w_intra = jnp.exp(li - m_t)
        w_inter = jnp.exp(inter - m_t)
        qk = _dot(c[:, sl], wqk_ref[h])
        q = qk[:, 0:HD]
        k = qk[:, HD:2 * HD] * (HD ** -0.5)
        v = mv_ref[:, sl].astype(F32)
        n0 = n_in[:, sl]
        s = jnp.sum(q * k, axis=1, keepdims=True) * w_intra
        kw = k * w_intra
        vb = mv_ref[:, sl]
        qC = jnp.zeros((TB, HD), F32)
        for j in range(TB):
            Cj = C_in[j, h]
            only_j = rows == j
            qC = qC + _dot(jnp.where(only_j, q, 0.0).astype(BF16), Cj.astype(BF16))
            outer = _dot_tn(jnp.where(only_j, kw, 0.0).astype(BF16), vb)
            C_out[j, h] = w_inter[j:j + 1, :] * Cj + outer
        num = w_inter * qC + s * v
        den = w_inter * jnp.sum(q * n0, axis=1, keepdims=True) + s
        hval = num / jnp.maximum(jnp.abs(den), jnp.exp(-m_t))
        n_out[:, sl] = w_inter * n0 + kw
        m_new_all = jnp.where(lane == h, m_t, m_new_all)
        ms = jnp.mean(hval * hval, axis=-1, keepdims=True)
        y = hval * lax.rsqrt(ms + NORM_EPS) * mg_ref[...]
        hm_ref[:, sl] = (y * jax.nn.sigmoid(mo_ref[:, sl])).astype(BF16)
    m_out[...] = m_new_all


def _mlstm_sample(u, conv_state, mv, mo, zif, cw, cb, wqk, bif, mg, C0, n0, m0, TB):
    Bd, W = u.shape
    H, HD = ML_HEADS, ML_HEAD_DIM
    row = lambda i: (i, 0)
    fixed = lambda i: (0, 0)
    m0p = jnp.pad(m0, ((0, 0), (0, LANES - H)))
    outs = [
        jax.ShapeDtypeStruct((Bd, W), BF16),
        jax.ShapeDtypeStruct((Bd, H, HD, HD), F32),
        jax.ShapeDtypeStruct((Bd, W), F32),
        jax.ShapeDtypeStruct((Bd, LANES), F32),
    ]
    return pl.pallas_call(
        functools.partial(_mlstm_sample_kernel, TB=TB),
        grid=(Bd // TB,),
        in_specs=[
            pl.BlockSpec((TB, W), row),
            pl.BlockSpec((TB, W), row),
            pl.BlockSpec((TB, W), row),
            pl.BlockSpec((TB, W), row),
            pl.BlockSpec((TB, W), row),
            pl.BlockSpec((TB, W), row),
            pl.BlockSpec((TB, LANES), row),
            pl.BlockSpec((CONV_W, W), fixed),
            pl.BlockSpec((1, W), fixed),
            pl.BlockSpec((H, HD, 2 * HD), lambda i: (0, 0, 0)),
            pl.BlockSpec((1, LANES), fixed),
            pl.BlockSpec((1, HD), fixed),
            pl.BlockSpec((TB, H, HD, HD), lambda i: (i, 0, 0, 0)),
            pl.BlockSpec((TB, W), row),
            pl.BlockSpec((TB, LANES), row),
        ],
        out_specs=[
            pl.BlockSpec((TB, W), row),
            pl.BlockSpec((TB, H, HD, HD), lambda i: (i, 0, 0, 0)),
            pl.BlockSpec((TB, W), row),
            pl.BlockSpec((TB, LANES), row),
        ],
        out_shape=outs,
        compiler_params=_cparams(("parallel",)),
        name="mlstm_sample",
    )(u, conv_state[:, 0], conv_state[:, 1], conv_state[:, 2], mv, mo, zif, cw, cb, wqk, bif, mg,
      C0, n0.reshape(Bd, W), m0p)


def _merge_kernel(x_ref, oa_ref, hm_ref, g1_ref, wg_ref, bm_ref, wa_ref, wb_ref, wo_ref, o_ref):
    x = x_ref[...]
    D = x.shape[1]
    ms = jnp.mean(x * x, axis=-1, keepdims=True)
    h = (x * lax.rsqrt(ms + NORM_EPS) * g1_ref[...]).astype(BF16)
    g = jax.nn.sigmoid(_dot(h, wg_ref[...]) + bm_ref[...])
    ya = _dot(oa_ref[...], wa_ref[...])
    yb = _dot(hm_ref[...], wb_ref[...])
    mix = (g[:, 0:D] * ya + g[:, D:2 * D] * yb).astype(BF16)
    o_ref[...] = x + _dot(mix, wo_ref[...])


def _merge(x2, oa, hm, g1, wg, bm, wa, wb, wo, tm):
    M, D = x2.shape
    W = oa.shape[1]
    row = lambda i: (i, 0)
    fixed = lambda i: (0, 0)
    return pl.pallas_call(
        _merge_kernel,
        grid=(M // tm,),
        in_specs=[
            pl.BlockSpec((tm, D), row),
            pl.BlockSpec((tm, W), row),
            pl.BlockSpec((tm, W), row),
            pl.BlockSpec((1, D), fixed),
            pl.BlockSpec(wg.shape, fixed),
            pl.BlockSpec((1, 2 * D), fixed),
            pl.BlockSpec(wa.shape, fixed),
            pl.BlockSpec(wb.shape, fixed),
            pl.BlockSpec(wo.shape, fixed),
        ],
        out_specs=pl.BlockSpec((tm, D), row),
        out_shape=jax.ShapeDtypeStruct((M, D), F32),
        compiler_params=_cparams(("parallel",)),
        name="merge",
    )(x2, oa, hm, g1, wg, bm, wa, wb, wo)


def _ffn_kernel(x_ref, g2_ref, wgate_ref, wup_ref, wdown_ref, o_ref):
    x = x_ref[...]
    ms = jnp.mean(x * x, axis=-1, keepdims=True)
    h = (x * lax.rsqrt(ms + NORM_EPS) * g2_ref[...]).astype(BF16)
    a = _dot(h, wgate_ref[...])
    b = _dot(h, wup_ref[...])
    act = (a * jax.nn.sigmoid(a) * b).astype(BF16)
    o_ref[...] = x + _dot(act, wdown_ref[...])


def _ffn(x2, g2, wgate, wup, wdown, tm):
    M, D = x2.shape
    row = lambda i: (i, 0)
    fixed = lambda i: (0, 0)
    return pl.pallas_call(
        _ffn_kernel,
        grid=(M // tm,),
        in_specs=[
            pl.BlockSpec((tm, D), row),
            pl.BlockSpec((1, D), fixed),
            pl.BlockSpec(wgate.shape, fixed),
            pl.BlockSpec(wup.shape, fixed),
            pl.BlockSpec(wdown.shape, fixed),
        ],
        out_specs=pl.BlockSpec((tm, D), row),
        out_shape=jax.ShapeDtypeStruct((M, D), F32),
        compiler_params=_cparams(("parallel",)),
        name="ffn",
    )(x2, g2, wgate, wup, wdown)


def _rope_tables(pos):
    half = DA_HEAD_DIM // 2
    inv = ROPE_THETA ** (-jnp.arange(half, dtype=F32) / half)
    ang = pos.astype(F32)[:, None] * inv[None, :]
    cos = jnp.cos(ang)
    sin = jnp.sin(ang)
    cos = jnp.concatenate([cos, cos], axis=1)
    sin = jnp.concatenate([-sin, sin], axis=1)
    reps = LANES // DA_HEAD_DIM
    return jnp.concatenate([cos] * reps, axis=1), jnp.concatenate([sin] * reps, axis=1)


def _pick_tile(M, pref):
    t = min(pref, M)
    while M % t:
        t //= 2
    return t


def _layer_weights(w_in, qnorm_g, knorm_g, lambda_q1, lambda_k1, lambda_q2, lambda_k2, subln_g, w_a_out,
                   conv_w, conv_b, w_mq, w_mk, b_igate, b_fgate, mnorm_g, w_b_out, b_merge, w_o,
                   w_ffn_gate, w_ffn_up, w_ffn_down):
    W = DA_HEADS * DA_V_DIM
    n_main = 6 * W
    n_if = 2 * ML_HEADS
    w_if = jnp.pad(w_in[:, n_main:n_main + n_if], ((0, 0), (0, LANES - n_if)))
    p = {}
    p['w_main'] = jnp.concatenate([w_in[:, :n_main], w_if], axis=1).astype(BF16)
    p['w_gates'] = w_in[:, n_main + n_if:].astype(BF16)
    p['qg'] = jnp.tile(qnorm_g, 2 * DA_HEADS)[None, :]
    p['kg'] = jnp.tile(knorm_g, 2 * DA_HEADS)[None, :]
    grp = np.arange(W) // DA_HEAD_DIM
    p['gm'] = jnp.asarray((grp[:, None] == grp[None, :]).astype(np.float32) / DA_HEAD_DIM, dtype=BF16)
    p['lams'] = jnp.stack([lambda_q1, lambda_k1, lambda_q2, lambda_k2]).astype(F32)
    p['sg'] = subln_g[None, :]
    p['w_a_out'] = w_a_out.astype(BF16)
    p['conv_w'] = conv_w
    p['conv_b'] = conv_b[None, :]
    p['wqk'] = jnp.concatenate([w_mq, w_mk], axis=2).astype(BF16)
    p['bif'] = jnp.pad(jnp.concatenate([b_igate, b_fgate]), (0, LANES - n_if))[None, :]
    p['mg'] = mnorm_g[None, :]
    p['w_b_out'] = w_b_out.astype(BF16)
    p['b_merge'] = b_merge[None, :]
    p['w_o'] = w_o.astype(BF16)
    p['w_ffn_gate'] = w_ffn_gate.astype(BF16)
    p['w_ffn_up'] = w_ffn_up.astype(BF16)
    p['w_ffn_down'] = w_ffn_down.astype(BF16)
    return p


def kernel(x_prompt, x_sample, cache_k, cache_v, page_table, state_C, state_n, state_m, state_conv, norm1_g, w_in, qnorm_g, knorm_g, lambda_q1, lambda_k1, lambda_q2, lambda_k2, subln_g, w_a_out, conv_w, conv_b, w_mq, w_mk, b_igate, b_fgate, mnorm_g, w_b_out, b_merge, w_o, norm2_g, w_ffn_gate, w_ffn_up, w_ffn_down):
    Bp, Tp, D = x_prompt.shape
    Bs, Ts, _ = x_sample.shape
    assert Ts == 1, "the sample group decodes one token per sequence"
    depth = w_in.shape[0]
    n_pages = page_table.shape[1]
    page_rows = cache_k.shape[2]
    past_len = n_pages * page_rows
    W = DA_HEADS * DA_V_DIM
    H, HD = ML_HEADS, ML_HEAD_DIM

    Mp = Bp * Tp
    tm_p = _pick_tile(Tp, 512)
    tm_s = _pick_tile(Bs, 128)
    cos_p, sin_p = _rope_tables(jnp.arange(Tp))
    cos_s, sin_s = _rope_tables(past_len + jnp.arange(Ts))
    cos_s = jnp.tile(cos_s, (tm_s, 1))
    sin_s = jnp.tile(sin_s, (tm_s, 1))
    tq = _pick_tile(Tp, 512)
    L = _pick_tile(Tp, 256)
    G = _pick_tile(n_pages, 8)
    TB = _pick_tile(Bs, 8)

    yp = x_prompt.reshape(Mp, D)
    ys = x_sample.reshape(Bs, D)
    outs = {k: [] for k in ('kp', 'vp', 'Cp', 'np', 'mp', 'cp', 'ks', 'vs', 'Cs', 'ns', 'ms', 'cs')}
    for l in range(depth):
        p = _layer_weights(w_in[l], qnorm_g[l], knorm_g[l], lambda_q1[l], lambda_k1[l], lambda_q2[l],
                           lambda_k2[l], subln_g[l], w_a_out[l], conv_w[l], conv_b[l], w_mq[l], w_mk[l],
                           b_igate[l], b_fgate[l], mnorm_g[l], w_b_out[l], b_merge[l], w_o[l],
                           w_ffn_gate[l], w_ffn_up[l], w_ffn_down[l])
        g1 = norm1_g[l][None, :]
        g2 = norm2_g[l][None, :]
        lam_init = 0.8 - 0.6 * math.exp(-0.3 * l)

        q, k, kb, v, vb, u, mv, mo, zif = _in_proj(yp, g1, p['w_main'], p['qg'], p['kg'], cos_p, sin_p,
                                                   p['gm'], tm_p)
        oa = _attn_prompt(q.reshape(Bp, Tp, W), kb.reshape(Bp, Tp, W), vb.reshape(Bp, Tp, W),
                          p['lams'], p['sg'], lam_init, tq)
        hm, Cp, np_, mp = _mlstm_prompt(u.reshape(Bp, Tp, W), mv.reshape(Bp, Tp, W), mo.reshape(Bp, Tp, W),
                                        zif.reshape(Bp, Tp, LANES), p['conv_w'], p['conv_b'], p['wqk'],
                                        p['bif'], p['mg'], L)
        x1 = _merge(yp, oa.reshape(Mp, W), hm.reshape(Mp, W), g1, p['w_gates'], p['b_merge'],
                    p['w_a_out'], p['w_b_out'], p['w_o'], tm_p)
        yp = _ffn(x1, g2, p['w_ffn_gate'], p['w_ffn_up'], p['w_ffn_down'], tm_p)
        outs['kp'].append(k.reshape(Bp, Tp, 2 * DA_HEADS, DA_HEAD_DIM))
        outs['vp'].append(v.reshape(Bp, Tp, DA_HEADS, DA_V_DIM))
        outs['Cp'].append(Cp)
        outs['np'].append(np_[:, :H, :])
        outs['mp'].append(mp[:, :H, 0])
        outs['cp'].append(u.reshape(Bp, Tp, W)[:, Tp - (CONV_W - 1):, :])

        q, k, kb, v, vb, u, mv, mo, zif = _in_proj(ys, g1, p['w_main'], p['qg'], p['kg'], cos_s, sin_s,
                                                   p['gm'], tm_s)
        ck = cache_k[l].reshape(cache_k.shape[1], page_rows, W)
        cv = cache_v[l].reshape(cache_v.shape[1], page_rows, W)
        oa = _attn_sample(q, k, v, ck, cv, page_table, p['lams'], p['sg'], lam_init, G)
        hm, Cs, ns, ms = _mlstm_sample(u, state_conv[l], mv, mo, zif, p['conv_w'], p['conv_b'], p['wqk'],
                                       p['bif'], p['mg'], state_C[l], state_n[l], state_m[l], TB)
        x1 = _merge(ys, oa, hm, g1, p['w_gates'], p['b_merge'], p['w_a_out'], p['w_b_out'], p['w_o'], tm_s)
        ys = _ffn(x1, g2, p['w_ffn_gate'], p['w_ffn_up'], p['w_ffn_down'], tm_s)
        outs['ks'].append(k.reshape(Bs, Ts, 2 * DA_HEADS, DA_HEAD_DIM))
        outs['vs'].append(v.reshape(Bs, Ts, DA_HEADS, DA_V_DIM))
        outs['Cs'].append(Cs)
        outs['ns'].append(ns.reshape(Bs, H, HD))
        outs['ms'].append(ms[:, :H])
        outs['cs'].append(jnp.concatenate([state_conv[l][:, 1:, :], u[:, None, :]], axis=1))

    st = lambda name: jnp.stack(outs[name])
    return (yp.reshape(Bp, Tp, D), ys.reshape(Bs, Ts, D),
            st('kp'), st('vp'), st('Cp'), st('np'), st('mp'), st('cp'),
            st('ks'), st('vs'), st('Cs'), st('ns'), st('ms'), st('cs'))
```

```python
import functools
import math

import jax
import jax.numpy as jnp
import numpy as np
from jax import lax
from jax.experimental import pallas as pl
from jax.experimental.pallas import tpu as pltpu

F32 = jnp.float32
BF16 = jnp.bfloat16

DA_HEADS = 4
DA_HEAD_DIM = 64
DA_V_DIM = 2 * DA_HEAD_DIM
ML_HEADS = 4
ML_HEAD_DIM = 128
CONV_W = 4
ROPE_THETA = 10000.0
NORM_EPS = 1e-6

LANES = 128
SUBLANES = 8
VMEM_LIMIT = 56 * 1024 * 1024

_HIGHEST = lax.Precision.HIGHEST


def _cparams(sem):
    return pltpu.CompilerParams(dimension_semantics=sem, vmem_limit_bytes=VMEM_LIMIT)


def _dot(a, b):
    return jnp.dot(a, b, preferred_element_type=F32)


def _dot_nt(a, b):
    return lax.dot_general(a, b, (((1,), (1,)), ((), ())), preferred_element_type=F32)


def _dot_tn(a, b):
    return lax.dot_general(a, b, (((0,), (0,)), ((), ())), preferred_element_type=F32)


def _log_sigmoid(x):
    return jnp.minimum(x, 0.0) - jnp.log(1.0 + jnp.exp(-jnp.abs(x)))


def _in_proj_kernel(x_ref, g1_ref, w_ref, qg_ref, kg_ref, cos_ref, sin_ref, gm_ref,
                    q_ref, k_ref, kb_ref, v_ref, vb_ref, u_ref, mv_ref, mo_ref, zif_ref):
    x = x_ref[...]
    ms = jnp.mean(x * x, axis=-1, keepdims=True)
    h = (x * lax.rsqrt(ms + NORM_EPS) * g1_ref[...]).astype(BF16)
    z = _dot(h, w_ref[...])
    width = DA_HEADS * DA_V_DIM
    cos = jnp.concatenate([cos_ref[...]] * (width // LANES), axis=1)
    sin = jnp.concatenate([sin_ref[...]] * (width // LANES), axis=1)
    lane = lax.broadcasted_iota(jnp.int32, (x.shape[0], width), 1)
    first_half = (lane % DA_HEAD_DIM) < (DA_HEAD_DIM // 2)

    def qk_norm_rope(t, g):
        ms_g = _dot((t * t).astype(BF16), gm_ref[...])
        y = t * lax.rsqrt(ms_g + NORM_EPS) * g
        partner = jnp.where(first_half,
                            pltpu.roll(y, width - DA_HEAD_DIM // 2, 1),
                            pltpu.roll(y, DA_HEAD_DIM // 2, 1))
        return y * cos + partner * sin

    q = qk_norm_rope(z[:, 0:width], qg_ref[...])
    k = qk_norm_rope(z[:, width:2 * width], kg_ref[...])
    q_ref[...] = (q * (DA_HEAD_DIM ** -0.5)).astype(q_ref.dtype)
    k_ref[...] = k
    kb_ref[...] = k.astype(BF16)
    v = z[:, 2 * width:3 * width]
    v_ref[...] = v
    vb_ref[...] = v.astype(BF16)
    u_ref[...] = z[:, 3 * width:4 * width]
    mv_ref[...] = z[:, 4 * width:5 * width].astype(BF16)
    mo_ref[...] = z[:, 5 * width:6 * width]
    zif_ref[...] = z[:, 6 * width:6 * width + LANES]


def _in_proj(x2, g1, w, qg, kg, cos, sin, gm, tm, q_dtype):
    M, D = x2.shape
    W = 512
    tab_blocks = cos.shape[0] // tm
    row = lambda i: (i, 0)
    tab = lambda i: (i % tab_blocks, 0)
    fixed = lambda i: (0, 0)
    outs = [
        jax.ShapeDtypeStruct((M, W), q_dtype),
        jax.ShapeDtypeStruct((M, W), F32),
        jax.ShapeDtypeStruct((M, W), BF16),
        jax.ShapeDtypeStruct((M, W), F32),
        jax.ShapeDtypeStruct((M, W), BF16),
        jax.ShapeDtypeStruct((M, W), F32),
        jax.ShapeDtypeStruct((M, W), BF16),
        jax.ShapeDtypeStruct((M, W), F32),
        jax.ShapeDtypeStruct((M, LANES), F32),
    ]
    return pl.pallas_call(
        _in_proj_kernel,
        grid=(M // tm,),
        in_specs=[
            pl.BlockSpec((tm, D), row),
            pl.BlockSpec((1, D), fixed),
            pl.BlockSpec(w.shape, fixed),
            pl.BlockSpec((1, W), fixed),
            pl.BlockSpec((1, W), fixed),
            pl.BlockSpec((tm, LANES), tab),
            pl.BlockSpec((tm, LANES), tab),
            pl.BlockSpec((W, W), fixed),
        ],
        out_specs=[pl.BlockSpec((tm, o.shape[1]), row) for o in outs],
        out_shape=outs,
        compiler_params=_cparams(("parallel",)),
        name="in_proj",
    )(x2, g1, w, qg, kg, cos, sin, gm)


def _lambda_value(lams, lam_init):
    a = jnp.sum(lams[0:1, :] * lams[1:2, :], axis=1, keepdims=True)
    b = jnp.sum(lams[2:3, :] * lams[3:4, :], axis=1, keepdims=True)
    return jnp.exp(a) - jnp.exp(b) + lam_init


def _lane_tile(x, n):
    return x if n == 1 else jnp.concatenate([x] * n, axis=1)


def _attn_prompt_kernel(qi_tab, ki_tab, lams_ref, q_ref, k_ref, v_ref, sg_ref, o_ref,
                        qs, vx, m_s, acc_s, *, tq, rq, lam_init):
    step = pl.program_id(2)
    qi = qi_tab[step]
    ki = ki_tab[step]
    tk = tq

    @pl.when(ki == 0)
    def _():
        q = q_ref[...].astype(F32)
        lane = lax.broadcasted_iota(jnp.int32, q.shape, 1)
        qs[0:tq, :] = jnp.where(lane < DA_HEAD_DIM, q, 0.0).astype(BF16)
        qs[tq:2 * tq, :] = jnp.where(lane >= DA_HEAD_DIM, q, 0.0).astype(BF16)
        vx[:, LANES:2 * LANES] = jnp.ones((tk, LANES), BF16)
        m_s[...] = jnp.full(m_s.shape, -jnp.inf, F32)
        acc_s[...] = jnp.zeros(acc_s.shape, F32)

    def update(masked):
        vx[:, 0:LANES] = v_ref[...]
        for r0 in range(0, 2 * tq, rq):
            off = r0 % tq
            nk = min(tk, off + rq) if masked else tk
            rows = slice(r0, r0 + rq)
            s = _dot_nt(qs[rows, :], k_ref[0:nk, :])
            if masked:
                row = lax.broadcasted_iota(jnp.int32, s.shape, 0) + off
                col = lax.broadcasted_iota(jnp.int32, s.shape, 1)
                s = jnp.where(col <= row, s, -jnp.inf)
            m_prev = m_s[rows, :]
            m_new = jnp.maximum(m_prev, jnp.max(s, axis=1, keepdims=True))
            alpha = jnp.exp(m_prev - m_new)
            p = jnp.exp(s - _lane_tile(m_new, nk // LANES))
            pv = _dot(p.astype(BF16), vx[0:nk, :])
            acc_s[rows, :] = _lane_tile(alpha, 2) * acc_s[rows, :] + pv
            m_s[rows, :] = m_new

    @pl.when(ki < qi)
    def _():
        update(False)

    @pl.when(ki == qi)
    def _():
        update(True)
        o = acc_s[:, 0:LANES] / acc_s[:, LANES:2 * LANES]
        lam = _lambda_value(lams_ref[...], lam_init)
        d = o[0:tq, :] - lam * o[tq:2 * tq, :]
        ms = jnp.mean(d * d, axis=-1, keepdims=True)
        y = d * lax.rsqrt(ms + NORM_EPS) * sg_ref[...] * (1.0 - lam_init)
        o_ref[...] = y.astype(BF16)


def _attn_prompt(q, k, v, lams, sg, lam_init, tq, rq):
    B, T, W = q.shape
    nq = T // tq
    qi_tab = np.concatenate([np.full(i + 1, i, np.int32) for i in range(nq)])
    ki_tab = np.concatenate([np.arange(i + 1, dtype=np.int32) for i in range(nq)])
    nsteps = int(qi_tab.shape[0])
    grid_spec = pltpu.PrefetchScalarGridSpec(
        num_scalar_prefetch=2,
        grid=(B, DA_HEADS, nsteps),
        in_specs=[
            pl.BlockSpec((4, DA_HEAD_DIM), lambda b, h, s, qt, kt: (0, 0)),
            pl.BlockSpec((None, tq, LANES), lambda b, h, s, qt, kt: (b, qt[s], h)),
            pl.BlockSpec((None, tq, LANES), lambda b, h, s, qt, kt: (b, kt[s], h)),
            pl.BlockSpec((None, tq, LANES), lambda b, h, s, qt, kt: (b, kt[s], h)),
            pl.BlockSpec((1, LANES), lambda b, h, s, qt, kt: (0, 0)),
        ],
        out_specs=pl.BlockSpec((None, tq, LANES), lambda b, h, s, qt, kt: (b, qt[s], h)),
        scratch_shapes=[
            pltpu.VMEM((2 * tq, LANES), BF16),
            pltpu.VMEM((tq, 2 * LANES), BF16),
            pltpu.VMEM((2 * tq, LANES), F32),
            pltpu.VMEM((2 * tq, 2 * LANES), F32),
        ],
    )
    return pl.pallas_call(
        functools.partial(_attn_prompt_kernel, tq=tq, rq=rq, lam_init=lam_init),
        grid_spec=grid_spec,
        out_shape=jax.ShapeDtypeStruct((B, T, W), BF16),
        compiler_params=_cparams(("parallel", "parallel", "arbitrary")),
        name="attn_prompt",
    )(jnp.asarray(qi_tab), jnp.asarray(ki_tab), lams, q, k, v, sg)


def _attn_sample_kernel(pt_ref, lams_ref, q_ref, kn_ref, vn_ref, sg_ref, *rest, G, lam_init):
    k_refs = rest[0:G]
    v_refs = rest[G:2 * G]
    o_ref = rest[2 * G]
    qcol, m_s, l_s, acc = rest[2 * G + 1:]
    step = pl.program_id(1)
    W = qcol.shape[0]
    n_maps = 2 * DA_HEADS
    n_lane_blocks = W // LANES

    @pl.when(step == 0)
    def _():
        qb = jnp.broadcast_to(q_ref[...], (LANES, W))
        for j in range(n_lane_blocks):
            qcol[j * LANES:(j + 1) * LANES, :] = jnp.transpose(qb[:, j * LANES:(j + 1) * LANES])
        m_s[...] = jnp.full(m_s.shape, -jnp.inf, F32)
        l_s[...] = jnp.zeros(l_s.shape, F32)
        acc[...] = jnp.zeros(acc.shape, F32)

    qc = qcol[...]
    s_list = []
    for i in range(G):
        prod = k_refs[i][...] * qc
        s_list.append(jnp.sum(prod.reshape(n_maps, DA_HEAD_DIM, LANES), axis=1))
    m_cur = functools.reduce(jnp.maximum, s_list)
    m_prev = m_s[...]
    m_new = jnp.maximum(m_prev, jnp.max(m_cur, axis=1, keepdims=True))
    alpha = jnp.exp(m_prev - m_new)
    p_list = [jnp.exp(s - m_new) for s in s_list]
    l_s[...] = alpha * l_s[...] + functools.reduce(jnp.add, p_list)
    p_all = jnp.concatenate(p_list, axis=1).astype(BF16)
    v_all = jnp.concatenate(
        [jnp.concatenate([v_refs[i][pl.ds(h, LANES, stride=DA_HEADS), :] for h in range(DA_HEADS)], axis=1)
         for i in range(G)], axis=0).astype(BF16)
    acc[...] = _lane_tile(alpha, n_lane_blocks) * acc[...] + _dot(p_all, v_all)
    m_s[...] = m_new

    @pl.when(step == pl.num_programs(1) - 1)
    def _():
        q8 = jnp.broadcast_to(q_ref[...], (n_maps, W))
        r = lax.broadcasted_iota(jnp.int32, (n_maps, W), 0)
        c = lax.broadcasted_iota(jnp.int32, (n_maps, W), 1)
        qm = jnp.where((c // DA_HEAD_DIM) == r, q8, 0.0)
        s_new = jnp.sum(qm * kn_ref[...], axis=1, keepdims=True)
        m_last = m_s[...]
        m_fin = jnp.maximum(m_last, s_new)
        a_fin = jnp.exp(m_last - m_fin)
        p_new = jnp.exp(s_new - m_fin)
        l_tot = jnp.sum(a_fin * l_s[...], axis=1, keepdims=True) + p_new[:, 0:1]
        o8 = (_lane_tile(a_fin, n_lane_blocks) * acc[...]
              + _lane_tile(p_new, n_lane_blocks) * vn_ref[...]) / l_tot
        lam = _lambda_value(lams_ref[...], lam_init)
        parts = []
        for h in range(DA_HEADS):
            blk = o8[:, h * DA_V_DIM:(h + 1) * DA_V_DIM]
            d = blk[2 * h:2 * h + 1, :] - lam * blk[2 * h + 1:2 * h + 2, :]
            ms = jnp.mean(d * d, axis=-1, keepdims=True)
            parts.append(d * lax.rsqrt(ms + NORM_EPS) * sg_ref[...] * (1.0 - lam_init))
        o_ref[...] = jnp.concatenate(parts, axis=1).astype(o_ref.dtype)


def _attn_sample(q, k_new, v_new, cache_kt, cache_v2, page_table, lams, sg, lam_init, G):
    Bd, W = q.shape
    NP = page_table.shape[1]
    assert cache_kt.shape[1:] == (W, LANES) and cache_v2.shape[1:] == (W, LANES)
    pt = page_table.reshape(-1)

    def page_map(i):
        return lambda b, s, pt: (pt[b * NP + s * G + i], 0, 0)

    per_seq = lambda b, s, pt: (b, 0, 0)
    fixed = lambda b, s, pt: (0, 0)
    grid_spec = pltpu.PrefetchScalarGridSpec(
        num_scalar_prefetch=1,
        grid=(Bd, NP // G),
        in_specs=[
            pl.BlockSpec((4, DA_HEAD_DIM), fixed),
            pl.BlockSpec((None, 1, W), per_seq),
            pl.BlockSpec((None, 1, W), per_seq),
            pl.BlockSpec((None, 1, W), per_seq),
            pl.BlockSpec((1, LANES), fixed),
        ] + [pl.BlockSpec((None, W, LANES), page_map(i)) for i in range(G)]
          + [pl.BlockSpec((None, W, LANES), page_map(i)) for i in range(G)],
        out_specs=pl.BlockSpec((None, 1, W), per_seq),
        scratch_shapes=[
            pltpu.VMEM((W, LANES), F32),
            pltpu.VMEM((2 * DA_HEADS, LANES), F32),
            pltpu.VMEM((2 * DA_HEADS, LANES), F32),
            pltpu.VMEM((2 * DA_HEADS, W), F32),
        ],
    )
    out = pl.pallas_call(
        functools.partial(_attn_sample_kernel, G=G, lam_init=lam_init),
        grid_spec=grid_spec,
        out_shape=jax.ShapeDtypeStruct((Bd, 1, W), BF16),
        compiler_params=_cparams(("parallel", "arbitrary")),
        name="attn_sample",
    )(pt, lams, q.reshape(Bd, 1, W), k_new.reshape(Bd, 1, W), v_new.reshape(Bd, 1, W), sg,
      *([cache_kt] * G), *([cache_v2] * G))
    return out.reshape(Bd, W)


def _gate_activations(zif, bif):
    g = zif + bif
    lane = lax.broadcasted_iota(jnp.int32, g.shape, 1)
    return jnp.where(lane < ML_HEADS, g, _log_sigmoid(g))


def _mlstm_prompt_kernel(u_ref, mv_ref, mo_ref, zif_ref, cw_ref, cb_ref, wqk_ref, bif_ref, mg_ref,
                         hm_ref, C_ref, n_ref, m_ref, ext, C_s, n_s, m_s, *, L):
    chunk = pl.program_id(1)
    HD = ML_HEAD_DIM

    @pl.when(chunk == 0)
    def _():
        ext[0:SUBLANES, :] = jnp.zeros((SUBLANES, ext.shape[1]), F32)
        C_s[...] = jnp.zeros(C_s.shape, F32)
        n_s[...] = jnp.zeros(n_s.shape, F32)
        m_s[...] = jnp.zeros(m_s.shape, F32)

    u = u_ref[...]
    ext[SUBLANES:SUBLANES + L, :] = u
    conv = cb_ref[...]
    for j in range(CONV_W):
        off = SUBLANES - (CONV_W - 1) + j
        conv = conv + ext[off:off + L, :] * cw_ref[j:j + 1, :]
    ext[0:SUBLANES, :] = u[L - SUBLANES:L, :]
    c = (conv * jax.nn.sigmoid(conv)).astype(BF16)

    gact = _gate_activations(zif_ref[...], bif_ref[...])
    gact_t = jnp.transpose(gact)
    row = lax.broadcasted_iota(jnp.int32, (L, L), 0)
    col = lax.broadcasted_iota(jnp.int32, (L, L), 1)
    causal = col <= row
    tri = causal.astype(F32)
    cum_col = jnp.dot(tri, gact, precision=_HIGHEST, preferred_element_type=F32)
    cum_row = lax.dot_general(gact_t[0:SUBLANES, :], tri, (((1,), (1,)), ((), ())),
                              precision=_HIGHEST, preferred_element_type=F32)

    for h in range(ML_HEADS):
        sl = slice(h * HD, (h + 1) * HD)
        b_col = cum_col[:, ML_HEADS + h:ML_HEADS + h + 1]
        b_row = cum_row[ML_HEADS + h:ML_HEADS + h + 1, :]
        li_col = gact[:, h:h + 1]
        li_row = gact_t[h:h + 1, :]
        m_prev = m_s[h:h + 1, 0:1]
        dmat = jnp.where(causal, b_col - b_row + li_row, -jnp.inf)
        inter = b_col + m_prev
        m_t = jnp.maximum(inter, jnp.max(dmat, axis=1, keepdims=True))
        w_intra = jnp.exp(dmat - m_t)
        w_inter = jnp.exp(inter - m_t)
        qk = _dot(c[:, sl], wqk_ref[h])
        q = qk[:, 0:HD]
        k = qk[:, HD:2 * HD] * (HD ** -0.5)
        qb = q.astype(BF16)
        v = mv_ref[:, sl]
        s = _dot_nt(qb, k.astype(BF16)) * w_intra
        C = C_s[h]
        n_row = n_s[h:h + 1, :]
        num = w_inter * _dot(qb, C.astype(BF16)) + _dot(s.astype(BF16), v)
        den = w_inter * jnp.sum(q * n_row, axis=1, keepdims=True) + jnp.sum(s, axis=1, keepdims=True)
        hval = num / jnp.maximum(jnp.abs(den), jnp.exp(-m_t))
        m_new = m_t[L - 1:L, :]
        b_last = b_col[L - 1:L, :]
        w_end = jnp.exp(b_last - b_col + li_col - m_new)
        decay = jnp.exp(b_last + m_prev - m_new)
        kw = k * w_end
        C_s[h] = decay * C + _dot_tn(kw.astype(BF16), v)
        n_s[h:h + 1, :] = decay * n_row + jnp.sum(kw, axis=0, keepdims=True)
        m_s[h:h + 1, :] = jnp.broadcast_to(m_new, (1, m_s.shape[1]))
        ms = jnp.mean(hval * hval, axis=-1, keepdims=True)
        y = hval * lax.rsqrt(ms + NORM_EPS) * mg_ref[...]
        hm_ref[:, sl] = (y * jax.nn.sigmoid(mo_ref[:, sl])).astype(BF16)

    @pl.when(chunk == pl.num_programs(1) - 1)
    def _():
        C_ref[...] = C_s[...]
        n_ref[...] = n_s[...]
        m_ref[...] = m_s[...]


def _mlstm_prompt(u, mv, mo, zif, cw, cb, wqk, bif, mg, L):
    B, T, W = u.shape
    H, HD = ML_HEADS, ML_HEAD_DIM
    tok = lambda b, c: (b, c, 0)
    fixed2 = lambda b, c: (0, 0)
    outs = [
        jax.ShapeDtypeStruct((B, T, W), BF16),
        jax.ShapeDtypeStruct((B, H, HD, HD), F32),
        jax.ShapeDtypeStruct((B, SUBLANES, HD), F32),
        jax.ShapeDtypeStruct((B, SUBLANES, LANES), F32),
    ]
    return pl.pallas_call(
        functools.partial(_mlstm_prompt_kernel, L=L),
        grid=(B, T // L),
        in_specs=[
            pl.BlockSpec((None, L, W), tok),
            pl.BlockSpec((None, L, W), tok),
            pl.BlockSpec((None, L, W), tok),
            pl.BlockSpec((None, L, LANES), tok),
            pl.BlockSpec((CONV_W, W), fixed2),
            pl.BlockSpec((1, W), fixed2),
            pl.BlockSpec((H, HD, 2 * HD), lambda b, c: (0, 0, 0)),
            pl.BlockSpec((1, LANES), fixed2),
            pl.BlockSpec((1, HD), fixed2),
        ],
        out_specs=[
            pl.BlockSpec((None, L, W), tok),
            pl.BlockSpec((None, H, HD, HD), lambda b, c: (b, 0, 0, 0)),
            pl.BlockSpec((None, SUBLANES, HD), lambda b, c: (b, 0, 0)),
            pl.BlockSpec((None, SUBLANES, LANES), lambda b, c: (b, 0, 0)),
        ],
        out_shape=outs,
        scratch_shapes=[
            pltpu.VMEM((L + SUBLANES, W), F32),
            pltpu.VMEM((H, HD, HD), F32),
            pltpu.VMEM((SUBLANES, HD), F32),
            pltpu.VMEM((SUBLANES, LANES), F32),
        ],
        compiler_params=_cparams(("parallel", "arbitrary")),
        name="mlstm_prompt",
    )(u, mv, mo, zif, cw, cb, wqk, bif, mg)


def _mlstm_sample_kernel(u_ref, c0_ref, c1_ref, c2_ref, mv_ref, mo_ref, zif_ref, cw_ref, cb_ref, wqk_ref,
                         bif_ref, mg_ref, C_in, n_in, m_in, hm_ref, C_out, n_out, m_out, *, TB):
    HD = ML_HEAD_DIM
    conv = (cb_ref[...] + c0_ref[...] * cw_ref[0:1, :] + c1_ref[...] * cw_ref[1:2, :]
            + c2_ref[...] * cw_ref[2:3, :] + u_ref[...] * cw_ref[3:4, :])
    c = (conv * jax.nn.sigmoid(conv)).astype(BF16)
    gact = _gate_activations(zif_ref[...], bif_ref[...])
    m0_all = m_in[...]
    rows = lax.broadcasted_iota(jnp.int32, (TB, HD), 0)
    m_new_all = jnp.zeros((TB, LANES), F32)
    lane = lax.broadcasted_iota(jnp.int32, (TB, LANES), 1)
    for h in range(ML_HEADS):
        sl = slice(h * HD, (h + 1) * HD)
        li = gact[:, h:h + 1]
        lf = gact[:, ML_HEADS + h:ML_HEADS + h + 1]
        m0 = m0_all[:, h:h + 1]
        inter = lf + m0
        m_t = jnp.maximum(inter, li)
        w_intra = jnp.exp(li - m_t)
        w_inter = jnp.exp(inter - m_t)
        qk = _dot(c[:, sl], wqk_ref[h])
        q = qk[:, 0:HD]
        k = qk[:, HD:2 * HD] * (HD ** -0.5)
        v = mv_ref[:, sl].astype(F32)
        n0 = n_in[:, sl]
        s = jnp.sum(q * k, axis=1, keepdims=True) * w_intra
        kw = k * w_intra
        vb = mv_ref[:, sl]
        qC = jnp.zeros((TB, HD), F32)
        for j in range(TB):
            Cj = C_in[j, h]
            only_j = rows == j
            qC = qC + _dot(jnp.where(only_j, q, 0.0).astype(BF16), Cj.astype(BF16))
            outer = _dot_tn(jnp.where(only_j, kw, 0.0).astype(BF16), vb)
            C_out[j, h] = w_inter[j:j + 1, :] * Cj + outer
        num = w_inter * qC + s * v
        den = w_inter * jnp.sum(q * n0, axis=1, keepdims=True) + s
        hval = num / jnp.maximum(jnp.abs(den), jnp.exp(-m_t))
        n_out[:, sl] = w_inter * n0 + kw
        m_new_all = jnp.where(lane == h, m_t, m_new_all)
        ms = jnp.mean(hval * hval, axis=-1, keepdims=True)
        y = hval * lax.rsqrt(ms + NORM_EPS) * mg_ref[...]
        hm_ref[:, sl] = (y * jax.nn.sigmoid(mo_ref[:, sl])).astype(BF16)
    m_out[...] = m_new_all


def _mlstm_sample(u, conv_state, mv, mo, zif, cw, cb, wqk, bif, mg, C0, n0, m0, TB):
    Bd, W = u.shape
    H, HD = ML_HEADS, ML_HEAD_DIM
    row = lambda i: (i, 0)
    fixed = lambda i: (0, 0)
    m0p = jnp.pad(m0, ((0, 0), (0, LANES - H)))
    outs = [
        jax.ShapeDtypeStruct((Bd, W), BF16),
        jax.ShapeDtypeStruct((Bd, H, HD, HD), F32),
        jax.ShapeDtypeStruct((Bd, W), F32),
        jax.ShapeDtypeStruct((Bd, LANES), F32),
    ]
    return pl.pallas_call(
        functools.partial(_mlstm_sample_kernel, TB=TB),
        grid=(Bd // TB,),
        in_specs=[
            pl.BlockSpec((TB, W), row),
            pl.BlockSpec((TB, W), row),
            pl.BlockSpec((TB, W), row),
            pl.BlockSpec((TB, W), row),
            pl.BlockSpec((TB, W), row),
            pl.BlockSpec((TB, W), row),
            pl.BlockSpec((TB, LANES), row),
            pl.BlockSpec((CONV_W, W), fixed),
            pl.BlockSpec((1, W), fixed),
            pl.BlockSpec((H, HD, 2 * HD), lambda i: (0, 0, 0)),
            pl.BlockSpec((1, LANES), fixed),
            pl.BlockSpec((1, HD), fixed),
            pl.BlockSpec((TB, H, HD, HD), lambda i: (i, 0, 0, 0)),
            pl.BlockSpec((TB, W), row),
            pl.BlockSpec((TB, LANES), row),
        ],
        out_specs=[
            pl.BlockSpec((TB, W), row),
            pl.BlockSpec((TB, H, HD, HD), lambda i: (i, 0, 0, 0)),
            pl.BlockSpec((TB, W), row),
            pl.BlockSpec((TB, LANES), row),
        ],
        out_shape=outs,
        compiler_params=_cparams(("parallel",)),
        name="mlstm_sample",
    )(u, conv_state[:, 0], conv_state[:, 1], conv_state[:, 2], mv, mo, zif, cw, cb, wqk, bif, mg,
      C0, n0.reshape(Bd, W), m0p)


def _merge_kernel(x_ref, oa_ref, hm_ref, g1_ref, wg_ref, bm_ref, wa_ref, wb_ref, wo_ref, o_ref):
    x = x_ref[...]
    D = x.shape[1]
    ms = jnp.mean(x * x, axis=-1, keepdims=True)
    h = (x * lax.rsqrt(ms + NORM_EPS) * g1_ref[...]).astype(BF16)
    g = jax.nn.sigmoid(_dot(h, wg_ref[...]) + bm_ref[...])
    ya = _dot(oa_ref[...], wa_ref[...])
    yb = _dot(hm_ref[...], wb_ref[...])
    mix = (g[:, 0:D] * ya + g[:, D:2 * D] * yb).astype(BF16)
    o_ref[...] = x + _dot(mix, wo_ref[...])


def _merge(x2, oa, hm, g1, wg, bm, wa, wb, wo, tm):
    M, D = x2.shape
    W = oa.shape[1]
    row = lambda i: (i, 0)
    fixed = lambda i: (0, 0)
    return pl.pallas_call(
        _merge_kernel,
        grid=(M // tm,),
        in_specs=[
            pl.BlockSpec((tm, D), row),
            pl.BlockSpec((tm, W), row),
            pl.BlockSpec((tm, W), row),
            pl.BlockSpec((1, D), fixed),
            pl.BlockSpec(wg.shape, fixed),
            pl.BlockSpec((1, 2 * D), fixed),
            pl.BlockSpec(wa.shape, fixed),
            pl.BlockSpec(wb.shape, fixed),
            pl.BlockSpec(wo.shape, fixed),
        ],
        out_specs=pl.BlockSpec((tm, D), row),
        out_shape=jax.ShapeDtypeStruct((M, D), F32),
        compiler_params=_cparams(("parallel",)),
        name="merge",
    )(x2, oa, hm, g1, wg, bm, wa, wb, wo)


def _ffn_kernel(x_ref, g2_ref, wgate_ref, wup_ref, wdown_ref, o_ref):
    x = x_ref[...]
    ms = jnp.mean(x * x, axis=-1, keepdims=True)
    h = (x * lax.rsqrt(ms + NORM_EPS) * g2_ref[...]).astype(BF16)
    a = _dot(h, wgate_ref[...])
    b = _dot(h, wup_ref[...])
    act = (a * jax.nn.sigmoid(a) * b).astype(BF16)
    o_ref[...] = x + _dot(act, wdown_ref[...])


def _ffn(x2, g2, wgate, wup, wdown, tm):
    M, D = x2.shape
    row = lambda i: (i, 0)
    fixed = lambda i: (0, 0)
    return pl.pallas_call(
        _ffn_kernel,
        grid=(M // tm,),
        in_specs=[
            pl.BlockSpec((tm, D), row),
            pl.BlockSpec((1, D), fixed),
            pl.BlockSpec(wgate.shape, fixed),
            pl.BlockSpec(wup.shape, fixed),
            pl.BlockSpec(wdown.shape, fixed),
        ],
        out_specs=pl.BlockSpec((tm, D), row),
        out_shape=jax.ShapeDtypeStruct((M, D), F32),
        compiler_params=_cparams(("parallel",)),
        name="ffn",
    )(x2, g2, wgate, wup, wdown)


def _rope_tables(pos):
    half = DA_HEAD_DIM // 2
    inv = ROPE_THETA ** (-jnp.arange(half, dtype=F32) / half)
    ang = pos.astype(F32)[:, None] * inv[None, :]
    cos = jnp.cos(ang)
    sin = jnp.sin(ang)
    cos = jnp.concatenate([cos, cos], axis=1)
    sin = jnp.concatenate([-sin, sin], axis=1)
    reps = LANES // DA_HEAD_DIM
    return jnp.concatenate([cos] * reps, axis=1), jnp.concatenate([sin] * reps, axis=1)


def _pick_tile(M, pref):
    t = min(pref, M)
    while M % t:
        t //= 2
    return t


def _layer_weights(w_in, qnorm_g, knorm_g, lambda_q1, lambda_k1, lambda_q2, lambda_k2, subln_g, w_a_out,
                   conv_w, conv_b, w_mq, w_mk, b_igate, b_fgate, mnorm_g, w_b_out, b_merge, w_o,
                   w_ffn_gate, w_ffn_up, w_ffn_down):
    W = DA_HEADS * DA_V_DIM
    n_main = 6 * W
    n_if = 2 * ML_HEADS
    w_if = jnp.pad(w_in[:, n_main:n_main + n_if], ((0, 0), (0, LANES - n_if)))
    p = {}
    p['w_main'] = jnp.concatenate([w_in[:, :n_main], w_if], axis=1).astype(BF16)
    p['w_gates'] = w_in[:, n_main + n_if:].astype(BF16)
    p['qg'] = jnp.tile(qnorm_g, 2 * DA_HEADS)[None, :]
    p['kg'] = jnp.tile(knorm_g, 2 * DA_HEADS)[None, :]
    grp = np.arange(W) // DA_HEAD_DIM
    p['gm'] = jnp.asarray((grp[:, None] == grp[None, :]).astype(np.float32) / DA_HEAD_DIM, dtype=BF16)
    p['lams'] = jnp.stack([lambda_q1, lambda_k1, lambda_q2, lambda_k2]).astype(F32)
    p['sg'] = subln_g[None, :]
    p['w_a_out'] = w_a_out.astype(BF16)
    p['conv_w'] = conv_w
    p['conv_b'] = conv_b[None, :]
    p['wqk'] = jnp.concatenate([w_mq, w_mk], axis=2).astype(BF16)
    p['bif'] = jnp.pad(jnp.concatenate([b_igate, b_fgate]), (0, LANES - n_if))[None, :]
    p['mg'] = mnorm_g[None, :]
    p['w_b_out'] = w_b_out.astype(BF16)
    p['b_merge'] = b_merge[None, :]
    p['w_o'] = w_o.astype(BF16)
    p['w_ffn_gate'] = w_ffn_gate.astype(BF16)
    p['w_ffn_up'] = w_ffn_up.astype(BF16)
    p['w_ffn_down'] = w_ffn_down.astype(BF16)
    return p


def kernel(x_prompt, x_sample, cache_k, cache_v, page_table, state_C, state_n, state_m, state_conv, norm1_g, w_in, qnorm_g, knorm_g, lambda_q1, lambda_k1, lambda_q2, lambda_k2, subln_g, w_a_out, conv_w, conv_b, w_mq, w_mk, b_igate, b_fgate, mnorm_g, w_b_out, b_merge, w_o, norm2_g, w_ffn_gate, w_ffn_up, w_ffn_down):
    Bp, Tp, D = x_prompt.shape
    Bs, Ts, _ = x_sample.shape
    assert Ts == 1, "the sample group decodes one token per sequence"
    depth = w_in.shape[0]
    n_pages = page_table.shape[1]
    page_rows = cache_k.shape[2]
    past_len = n_pages * page_rows
    W = DA_HEADS * DA_V_DIM
    H, HD = ML_HEADS, ML_HEAD_DIM

    Mp = Bp * Tp
    tm_p = _pick_tile(Tp, 512)
    tm_s = _pick_tile(Bs, 128)
    cos_p, sin_p = _rope_tables(jnp.arange(Tp))
    cos_s, sin_s = _rope_tables(past_len + jnp.arange(Ts))
    cos_s = jnp.tile(cos_s, (tm_s, 1))
    sin_s = jnp.tile(sin_s, (tm_s, 1))
    tq = _pick_tile(Tp, 1024)
    rq = min(tq, 256)
    L = _pick_tile(Tp, 256)
    G = _pick_tile(n_pages, 16)
    TB = _pick_tile(Bs, 8)

    yp = x_prompt.reshape(Mp, D)
    ys = x_sample.reshape(Bs, D)
    outs = {k: [] for k in ('kp', 'vp', 'Cp', 'np', 'mp', 'cp', 'ks', 'vs', 'Cs', 'ns', 'ms', 'cs')}
    for l in range(depth):
        p = _layer_weights(w_in[l], qnorm_g[l], knorm_g[l], lambda_q1[l], lambda_k1[l], lambda_q2[l],
                           lambda_k2[l], subln_g[l], w_a_out[l], conv_w[l], conv_b[l], w_mq[l], w_mk[l],
                           b_igate[l], b_fgate[l], mnorm_g[l], w_b_out[l], b_merge[l], w_o[l],
                           w_ffn_gate[l], w_ffn_up[l], w_ffn_down[l])
        g1 = norm1_g[l][None, :]
        g2 = norm2_g[l][None, :]
        lam_init = 0.8 - 0.6 * math.exp(-0.3 * l)

        q, k, kb, v, vb, u, mv, mo, zif = _in_proj(yp, g1, p['w_main'], p['qg'], p['kg'], cos_p, sin_p,
                                                   p['gm'], tm_p, BF16)
        oa = _attn_prompt(q.reshape(Bp, Tp, W), kb.reshape(Bp, Tp, W), vb.reshape(Bp, Tp, W),
                          p['lams'], p['sg'], lam_init, tq, rq)
        hm, Cp, np_, mp = _mlstm_prompt(u.reshape(Bp, Tp, W), mv.reshape(Bp, Tp, W), mo.reshape(Bp, Tp, W),
                                        zif.reshape(Bp, Tp, LANES), p['conv_w'], p['conv_b'], p['wqk'],
                                        p['bif'], p['mg'], L)
        x1 = _merge(yp, oa.reshape(Mp, W), hm.reshape(Mp, W), g1, p['w_gates'], p['b_merge'],
                    p['w_a_out'], p['w_b_out'], p['w_o'], tm_p)
        yp = _ffn(x1, g2, p['w_ffn_gate'], p['w_ffn_up'], p['w_ffn_down'], tm_p)
        outs['kp'].append(k.reshape(Bp, Tp, 2 * DA_HEADS, DA_HEAD_DIM))
        outs['vp'].append(v.reshape(Bp, Tp, DA_HEADS, DA_V_DIM))
        outs['Cp'].append(Cp)
        outs['np'].append(np_[:, :H, :])
        outs['mp'].append(mp[:, :H, 0])
        outs['cp'].append(u.reshape(Bp, Tp, W)[:, Tp - (CONV_W - 1):, :])

        q, k, kb, v, vb, u, mv, mo, zif = _in_proj(ys, g1, p['w_main'], p['qg'], p['kg'], cos_s, sin_s,
                                                   p['gm'], tm_s, F32)
        n_phys = cache_k.shape[1]
        ckt = jnp.transpose(cache_k[l], (0, 2, 3, 1)).reshape(n_phys, W, page_rows)
        cv2 = cache_v[l].reshape(n_phys, page_rows * DA_HEADS, DA_V_DIM)
        oa = _attn_sample(q, k, v, ckt, cv2, page_table, p['lams'], p['sg'], lam_init, G)
        hm, Cs, ns, ms = _mlstm_sample(u, state_conv[l], mv, mo, zif, p['conv_w'], p['conv_b'], p['wqk'],
                                       p['bif'], p['mg'], state_C[l], state_n[l], state_m[l], TB)
        x1 = _merge(ys, oa, hm, g1, p['w_gates'], p['b_merge'], p['w_a_out'], p['w_b_out'], p['w_o'], tm_s)
        ys = _ffn(x1, g2, p['w_ffn_gate'], p['w_ffn_up'], p['w_ffn_down'], tm_s)
        outs['ks'].append(k.reshape(Bs, Ts, 2 * DA_HEADS, DA_HEAD_DIM))
        outs['vs'].append(v.reshape(Bs, Ts, DA_HEADS, DA_V_DIM))
        outs['Cs'].append(Cs)
        outs['ns'].append(ns.reshape(Bs, H, HD))
        outs['ms'].append(ms[:, :H])
        outs['cs'].append(jnp.concatenate([state_conv[l][:, 1:, :], u[:, None, :]], axis=1))

    st = lambda name: jnp.stack(outs[name])
    return (yp.reshape(Bp, Tp, D), ys.reshape(Bs, Ts, D),
            st('kp'), st('vp'), st('Cp'), st('np'), st('mp'), st('cp'),
            st('ks'), st('vs'), st('Cs'), st('ns'), st('ms'), st('cs'))
```

```python
import functools
import math

import jax
import jax.numpy as jnp
import numpy as np
from jax import lax
from jax.experimental import pallas as pl
from jax.experimental.pallas import tpu as pltpu

F32 = jnp.float32
BF16 = jnp.bfloat16

DA_HEADS = 4
DA_HEAD_DIM = 64
DA_V_DIM = 2 * DA_HEAD_DIM
ML_HEADS = 4
ML_HEAD_DIM = 128
CONV_W = 4
ROPE_THETA = 10000.0
NORM_EPS = 1e-6

LANES = 128
SUBLANES = 8
VMEM_LIMIT = 56 * 1024 * 1024

_HIGHEST = lax.Precision.HIGHEST


def _cparams(sem):
    return pltpu.CompilerParams(dimension_semantics=sem, vmem_limit_bytes=VMEM_LIMIT)


def _dot(a, b):
    return jnp.dot(a, b, preferred_element_type=F32)


def _dot_nt(a, b):
    return lax.dot_general(a, b, (((1,), (1,)), ((), ())), preferred_element_type=F32)


def _dot_tn(a, b):
    return lax.dot_general(a, b, (((0,), (0,)), ((), ())), preferred_element_type=F32)


def _log_sigmoid(x):
    return jnp.minimum(x, 0.0) - jnp.log(1.0 + jnp.exp(-jnp.abs(x)))


def _in_proj_kernel(x_ref, g1_ref, w_ref, qg_ref, kg_ref, cos_ref, sin_ref, gm_ref,
                    q_ref, k_ref, kb_ref, v_ref, vb_ref, u_ref, mv_ref, mo_ref, zif_ref, *, cache_layout):
    x = x_ref[...]
    ms = jnp.mean(x * x, axis=-1, keepdims=True)
    h = (x * lax.rsqrt(ms + NORM_EPS) * g1_ref[...]).astype(BF16)
    z = _dot(h, w_ref[...])
    width = DA_HEADS * DA_V_DIM
    cos = jnp.concatenate([cos_ref[...]] * (width // LANES), axis=1)
    sin = jnp.concatenate([sin_ref[...]] * (width // LANES), axis=1)
    lane = lax.broadcasted_iota(jnp.int32, (x.shape[0], width), 1)
    first_half = (lane % DA_HEAD_DIM) < (DA_HEAD_DIM // 2)

    def qk_norm_rope(t, g):
        ms_g = _dot((t * t).astype(BF16), gm_ref[...])
        y = t * lax.rsqrt(ms_g + NORM_EPS) * g
        partner = jnp.where(first_half,
                            pltpu.roll(y, width - DA_HEAD_DIM // 2, 1),
                            pltpu.roll(y, DA_HEAD_DIM // 2, 1))
        return y * cos + partner * sin

    q = qk_norm_rope(z[:, 0:width], qg_ref[...])
    k = qk_norm_rope(z[:, width:2 * width], kg_ref[...])
    q_ref[...] = (q * (DA_HEAD_DIM ** -0.5)).astype(q_ref.dtype)
    v = z[:, 2 * width:3 * width]
    if cache_layout:
        tm = x.shape[0]
        for j in range(width // LANES):
            k_ref[j * LANES:(j + 1) * LANES, :] = jnp.transpose(k[:, j * LANES:(j + 1) * LANES])
        for hd in range(DA_HEADS):
            v_ref[pl.ds(hd, tm, stride=DA_HEADS), :] = v[:, hd * DA_V_DIM:(hd + 1) * DA_V_DIM]
    else:
        k_ref[...] = k
        v_ref[...] = v
    kb_ref[...] = k.astype(BF16)
    vb_ref[...] = v.astype(BF16)
    u_ref[...] = z[:, 3 * width:4 * width]
    mv_ref[...] = z[:, 4 * width:5 * width].astype(BF16)
    mo_ref[...] = z[:, 5 * width:6 * width]
    zif_ref[...] = z[:, 6 * width:6 * width + LANES]


def _in_proj(x2, g1, w, qg, kg, cos, sin, gm, tm, q_dtype, cache_layout):
    M, D = x2.shape
    W = 512
    tab_blocks = cos.shape[0] // tm
    row = lambda i: (i, 0)
    tab = lambda i: (i % tab_blocks, 0)
    fixed = lambda i: (0, 0)
    if cache_layout:
        R = cos.shape[0]
        k_out = jax.ShapeDtypeStruct((M // R, W, R), F32)
        k_spec = pl.BlockSpec((None, W, tm), lambda i: (i // tab_blocks, 0, i % tab_blocks))
        v_out = jax.ShapeDtypeStruct((M * DA_HEADS, DA_V_DIM), F32)
        v_spec = pl.BlockSpec((tm * DA_HEADS, DA_V_DIM), row)
    else:
        k_out = v_out = jax.ShapeDtypeStruct((M, W), F32)
        k_spec = v_spec = pl.BlockSpec((tm, W), row)
    outs = [
        jax.ShapeDtypeStruct((M, W), q_dtype),
        k_out,
        jax.ShapeDtypeStruct((M, W), BF16),
        v_out,
        jax.ShapeDtypeStruct((M, W), BF16),
        jax.ShapeDtypeStruct((M, W), F32),
        jax.ShapeDtypeStruct((M, W), BF16),
        jax.ShapeDtypeStruct((M, W), F32),
        jax.ShapeDtypeStruct((M, LANES), F32),
    ]
    out_specs = [pl.BlockSpec((tm, o.shape[1]), row) for o in outs]
    out_specs[1] = k_spec
    out_specs[3] = v_spec
    return pl.pallas_call(
        functools.partial(_in_proj_kernel, cache_layout=cache_layout),
        grid=(M // tm,),
        in_specs=[
            pl.BlockSpec((tm, D), row),
            pl.BlockSpec((1, D), fixed),
            pl.BlockSpec(w.shape, fixed),
            pl.BlockSpec((1, W), fixed),
            pl.BlockSpec((1, W), fixed),
            pl.BlockSpec((tm, LANES), tab),
            pl.BlockSpec((tm, LANES), tab),
            pl.BlockSpec((W, W), fixed),
        ],
        out_specs=out_specs,
        out_shape=outs,
        compiler_params=_cparams(("parallel",)),
        name="in_proj",
    )(x2, g1, w, qg, kg, cos, sin, gm)


def _lambda_value(lams, lam_init):
    a = jnp.sum(lams[0:1, :] * lams[1:2, :], axis=1, keepdims=True)
    b = jnp.sum(lams[2:3, :] * lams[3:4, :], axis=1, keepdims=True)
    return jnp.exp(a) - jnp.exp(b) + lam_init


def _lane_tile(x, n):
    return x if n == 1 else jnp.concatenate([x] * n, axis=1)


def _attn_prompt_kernel(qi_tab, ki_tab, lams_ref, q_ref, k_ref, v_ref, sg_ref, o_ref,
                        qs, vx, m_s, acc_s, *, tq, rq, lam_init):
    step = pl.program_id(2)
    qi = qi_tab[step]
    ki = ki_tab[step]
    tk = tq

    @pl.when(ki == 0)
    def _():
        q = q_ref[...].astype(F32)
        lane = lax.broadcasted_iota(jnp.int32, q.shape, 1)
        qs[0:tq, :] = jnp.where(lane < DA_HEAD_DIM, q, 0.0).astype(BF16)
        qs[tq:2 * tq, :] = jnp.where(lane >= DA_HEAD_DIM, q, 0.0).astype(BF16)
        vx[:, LANES:2 * LANES] = jnp.ones((tk, LANES), BF16)
        m_s[...] = jnp.full(m_s.shape, -jnp.inf, F32)
        acc_s[...] = jnp.zeros(acc_s.shape, F32)

    def update(masked):
        vx[:, 0:LANES] = v_ref[...]
        for r0 in range(0, 2 * tq, rq):
            off = r0 % tq
            nk = min(tk, off + rq) if masked else tk
            rows = slice(r0, r0 + rq)
            s = _dot_nt(qs[rows, :], k_ref[0:nk, :])
            if masked:
                row = lax.broadcasted_iota(jnp.int32, s.shape, 0) + off
                col = lax.broadcasted_iota(jnp.int32, s.shape, 1)
                s = jnp.where(col <= row, s, -jnp.inf)
            m_prev = m_s[rows, :]
            m_new = jnp.maximum(m_prev, jnp.max(s, axis=1, keepdims=True))
            alpha = jnp.exp(m_prev - m_new)
            p = jnp.exp(s - _lane_tile(m_new, nk // LANES))
            pv = _dot(p.astype(BF16), vx[0:nk, :])
            acc_s[rows, :] = _lane_tile(alpha, 2) * acc_s[rows, :] + pv
            m_s[rows, :] = m_new

    @pl.when(ki < qi)
    def _():
        update(False)

    @pl.when(ki == qi)
    def _():
        update(True)
        o = acc_s[:, 0:LANES] / acc_s[:, LANES:2 * LANES]
        lam = _lambda_value(lams_ref[...], lam_init)
        d = o[0:tq, :] - lam * o[tq:2 * tq, :]
        ms = jnp.mean(d * d, axis=-1, keepdims=True)
        y = d * lax.rsqrt(ms + NORM_EPS) * sg_ref[...] * (1.0 - lam_init)
        o_ref[...] = y.astype(BF16)


def _attn_prompt(q, k, v, lams, sg, lam_init, tq, rq):
    B, T, W = q.shape
    nq = T // tq
    qi_tab = np.concatenate([np.full(i + 1, i, np.int32) for i in range(nq)])
    ki_tab = np.concatenate([np.arange(i + 1, dtype=np.int32) for i in range(nq)])
    nsteps = int(qi_tab.shape[0])
    grid_spec = pltpu.PrefetchScalarGridSpec(
        num_scalar_prefetch=2,
        grid=(B, DA_HEADS, nsteps),
        in_specs=[
            pl.BlockSpec((4, DA_HEAD_DIM), lambda b, h, s, qt, kt: (0, 0)),
            pl.BlockSpec((None, tq, LANES), lambda b, h, s, qt, kt: (b, qt[s], h)),
            pl.BlockSpec((None, tq, LANES), lambda b, h, s, qt, kt: (b, kt[s], h)),
            pl.BlockSpec((None, tq, LANES), lambda b, h, s, qt, kt: (b, kt[s], h)),
            pl.BlockSpec((1, LANES), lambda b, h, s, qt, kt: (0, 0)),
        ],
        out_specs=pl.BlockSpec((None, tq, LANES), lambda b, h, s, qt, kt: (b, qt[s], h)),
        scratch_shapes=[
            pltpu.VMEM((2 * tq, LANES), BF16),
            pltpu.VMEM((tq, 2 * LANES), BF16),
            pltpu.VMEM((2 * tq, LANES), F32),
            pltpu.VMEM((2 * tq, 2 * LANES), F32),
        ],
    )
    return pl.pallas_call(
        functools.partial(_attn_prompt_kernel, tq=tq, rq=rq, lam_init=lam_init),
        grid_spec=grid_spec,
        out_shape=jax.ShapeDtypeStruct((B, T, W), BF16),
        compiler_params=_cparams(("parallel", "parallel", "arbitrary")),
        name="attn_prompt",
    )(jnp.asarray(qi_tab), jnp.asarray(ki_tab), lams, q, k, v, sg)


def _attn_sample_kernel(pt_ref, lams_ref, q_ref, kn_ref, vn_ref, sg_ref, *rest, G, lam_init):
    k_refs = rest[0:G]
    v_refs = rest[G:2 * G]
    o_ref = rest[2 * G]
    qcol, m_s, l_s, acc = rest[2 * G + 1:]
    step = pl.program_id(1)
    W = qcol.shape[0]
    n_maps = 2 * DA_HEADS
    n_lane_blocks = W // LANES

    @pl.when(step == 0)
    def _():
        qb = jnp.broadcast_to(q_ref[...], (LANES, W))
        for j in range(n_lane_blocks):
            qcol[j * LANES:(j + 1) * LANES, :] = jnp.transpose(qb[:, j * LANES:(j + 1) * LANES])
        m_s[...] = jnp.full(m_s.shape, -jnp.inf, F32)
        l_s[...] = jnp.zeros(l_s.shape, F32)
        acc[...] = jnp.zeros(acc.shape, F32)

    qc = qcol[...]
    s_list = []
    for i in range(G):
        prod = k_refs[i][...] * qc
        s_list.append(jnp.sum(prod.reshape(n_maps, DA_HEAD_DIM, LANES), axis=1))
    m_cur = functools.reduce(jnp.maximum, s_list)
    m_prev = m_s[...]
    m_new = jnp.maximum(m_prev, jnp.max(m_cur, axis=1, keepdims=True))
    alpha = jnp.exp(m_prev - m_new)
    p_list = [jnp.exp(s - m_new) for s in s_list]
    l_s[...] = alpha * l_s[...] + functools.reduce(jnp.add, p_list)
    p_all = jnp.concatenate(p_list, axis=1).astype(BF16)
    v_all = jnp.concatenate(
        [jnp.concatenate([v_refs[i][pl.ds(h, LANES, stride=DA_HEADS), :] for h in range(DA_HEADS)], axis=1)
         for i in range(G)], axis=0).astype(BF16)
    acc[...] = _lane_tile(alpha, n_lane_blocks) * acc[...] + _dot(p_all, v_all)
    m_s[...] = m_new

    @pl.when(step == pl.num_programs(1) - 1)
    def _():
        q8 = jnp.broadcast_to(q_ref[...], (n_maps, W))
        r = lax.broadcasted_iota(jnp.int32, (n_maps, W), 0)
        c = lax.broadcasted_iota(jnp.int32, (n_maps, W), 1)
        qm = jnp.where((c // DA_HEAD_DIM) == r, q8, 0.0)
        s_new = jnp.sum(qm * kn_ref[...], axis=1, keepdims=True)
        m_last = m_s[...]
        m_fin = jnp.maximum(m_last, s_new)
        a_fin = jnp.exp(m_last - m_fin)
        p_new = jnp.exp(s_new - m_fin)
        l_tot = jnp.sum(a_fin * l_s[...], axis=1, keepdims=True) + p_new[:, 0:1]
        o8 = (_lane_tile(a_fin, n_lane_blocks) * acc[...]
              + _lane_tile(p_new, n_lane_blocks) * vn_ref[...]) / l_tot
        lam = _lambda_value(lams_ref[...], lam_init)
        parts = []
        for h in range(DA_HEADS):
            blk = o8[:, h * DA_V_DIM:(h + 1) * DA_V_DIM]
            d = blk[2 * h:2 * h + 1, :] - lam * blk[2 * h + 1:2 * h + 2, :]
            ms = jnp.mean(d * d, axis=-1, keepdims=True)
            parts.append(d * lax.rsqrt(ms + NORM_EPS) * sg_ref[...] * (1.0 - lam_init))
        o_ref[...] = jnp.concatenate(parts, axis=1).astype(o_ref.dtype)


def _attn_sample(q, k_new, v_new, cache_kt, cache_v2, page_table, lams, sg, lam_init, G):
    Bd, W = q.shape
    NP = page_table.shape[1]
    assert cache_kt.shape[1:] == (W, LANES) and cache_v2.shape[1:] == (W, LANES)
    pt = page_table.reshape(-1)

    def page_map(i):
        return lambda b, s, pt: (pt[b * NP + s * G + i], 0, 0)

    per_seq = lambda b, s, pt: (b, 0, 0)
    fixed = lambda b, s, pt: (0, 0)
    grid_spec = pltpu.PrefetchScalarGridSpec(
        num_scalar_prefetch=1,
        grid=(Bd, NP // G),
        in_specs=[
            pl.BlockSpec((4, DA_HEAD_DIM), fixed),
            pl.BlockSpec((None, 1, W), per_seq),
            pl.BlockSpec((None, 1, W), per_seq),
            pl.BlockSpec((None, 1, W), per_seq),
            pl.BlockSpec((1, LANES), fixed),
        ] + [pl.BlockSpec((None, W, LANES), page_map(i)) for i in range(G)]
          + [pl.BlockSpec((None, W, LANES), page_map(i)) for i in range(G)],
        out_specs=pl.BlockSpec((None, 1, W), per_seq),
        scratch_shapes=[
            pltpu.VMEM((W, LANES), F32),
            pltpu.VMEM((2 * DA_HEADS, LANES), F32),
            pltpu.VMEM((2 * DA_HEADS, LANES), F32),
            pltpu.VMEM((2 * DA_HEADS, W), F32),
        ],
    )
    out = pl.pallas_call(
        functools.partial(_attn_sample_kernel, G=G, lam_init=lam_init),
        grid_spec=grid_spec,
        out_shape=jax.ShapeDtypeStruct((Bd, 1, W), BF16),
        compiler_params=_cparams(("parallel", "arbitrary")),
        name="attn_sample",
    )(pt, lams, q.reshape(Bd, 1, W), k_new.reshape(Bd, 1, W), v_new.reshape(Bd, 1, W), sg,
      *([cache_kt] * G), *([cache_v2] * G))
    return out.reshape(Bd, W)


def _gate_activations(zif, bif):
    g = zif + bif
    lane = lax.broadcasted_iota(jnp.int32, g.shape, 1)
    return jnp.where(lane < ML_HEADS, g, _log_sigmoid(g))


def _mlstm_prompt_kernel(u_ref, mv_ref, mo_ref, zif_ref, cw_ref, cb_ref, wqk_ref, bif_ref, mg_ref,
                         hm_ref, C_ref, n_ref, m_ref, ext, C_s, n_s, m_s, *, L, NB):
    chunk = pl.program_id(1)

    @pl.when(chunk == 0)
    def _():
        ext[:, 0:SUBLANES, :] = jnp.zeros((NB, SUBLANES, ext.shape[2]), F32)
        C_s[...] = jnp.zeros(C_s.shape, F32)
        n_s[...] = jnp.zeros(n_s.shape, F32)
        m_s[...] = jnp.zeros(m_s.shape, F32)

    for bi in range(NB):
        _mlstm_chunk(u_ref.at[bi], mv_ref.at[bi], mo_ref.at[bi], zif_ref.at[bi], cw_ref, cb_ref, wqk_ref,
                     bif_ref, mg_ref, hm_ref.at[bi], ext.at[bi], C_s.at[bi], n_s.at[bi], m_s.at[bi], L)

    @pl.when(chunk == pl.num_programs(1) - 1)
    def _():
        C_ref[...] = C_s[...]
        n_ref[...] = n_s[...]
        m_ref[...] = m_s[...]


def _mlstm_chunk(u_ref, mv_ref, mo_ref, zif_ref, cw_ref, cb_ref, wqk_ref, bif_ref, mg_ref,
                 hm_ref, ext, C_s, n_s, m_s, L):
    HD = ML_HEAD_DIM
    u = u_ref[...]
    ext[SUBLANES:SUBLANES + L, :] = u
    conv = cb_ref[...]
    for j in range(CONV_W):
        off = SUBLANES - (CONV_W - 1) + j
        conv = conv + ext[off:off + L, :] * cw_ref[j:j + 1, :]
    ext[0:SUBLANES, :] = u[L - SUBLANES:L, :]
    c = (conv * jax.nn.sigmoid(conv)).astype(BF16)

    gact = _gate_activations(zif_ref[...], bif_ref[...])
    gact_t = jnp.transpose(gact)
    row = lax.broadcasted_iota(jnp.int32, (L, L), 0)
    col = lax.broadcasted_iota(jnp.int32, (L, L), 1)
    causal = col <= row
    tri = causal.astype(F32)
    cum_col = jnp.dot(tri, gact, precision=_HIGHEST, preferred_element_type=F32)
    cum_row = lax.dot_general(gact_t[0:SUBLANES, :], tri, (((1,), (1,)), ((), ())),
                              precision=_HIGHEST, preferred_element_type=F32)

    for h in range(ML_HEADS):
        sl = slice(h * HD, (h + 1) * HD)
        b_col = cum_col[:, ML_HEADS + h:ML_HEADS + h + 1]
        b_row = cum_row[ML_HEADS + h:ML_HEADS + h + 1, :]
        li_col = gact[:, h:h + 1]
        li_row = gact_t[h:h + 1, :]
        m_prev = m_s[h:h + 1, 0:1]
        dmat = jnp.where(causal, b_col - b_row + li_row, -jnp.inf)
        inter = b_col + m_prev
        m_t = jnp.maximum(inter, jnp.max(dmat, axis=1, keepdims=True))
        w_intra = jnp.exp(dmat - m_t)
        w_inter = jnp.exp(inter - m_t)
        qk = _dot(c[:, sl], wqk_ref[h])
        q = qk[:, 0:HD]
        k = qk[:, HD:2 * HD] * (HD ** -0.5)
        qb = q.astype(BF16)
        v = mv_ref[:, sl]
        s = _dot_nt(qb, k.astype(BF16)) * w_intra
        C = C_s[h]
        n_row = n_s[h:h + 1, :]
        num = w_inter * _dot(qb, C.astype(BF16)) + _dot(s.astype(BF16), v)
        den = w_inter * jnp.sum(q * n_row, axis=1, keepdims=True) + jnp.sum(s, axis=1, keepdims=True)
        hval = num / jnp.maximum(jnp.abs(den), jnp.exp(-m_t))
        m_new = m_t[L - 1:L, :]
        b_last = b_col[L - 1:L, :]
        w_end = jnp.exp(b_last - b_col + li_col - m_new)
        decay = jnp.exp(b_last + m_prev - m_new)
        kw = k * w_end
        C_s[h] = decay * C + _dot_tn(kw.astype(BF16), v)
        n_s[h:h + 1, :] = decay * n_row + jnp.sum(kw, axis=0, keepdims=True)
        m_s[h:h + 1, :] = jnp.broadcast_to(m_new, (1, m_s.shape[1]))
        ms = jnp.mean(hval * hval, axis=-1, keepdims=True)
        y = hval * lax.rsqrt(ms + NORM_EPS) * mg_ref[...]
        hm_ref[:, sl] = (y * jax.nn.sigmoid(mo_ref[:, sl])).astype(BF16)


def _mlstm_prompt(u, mv, mo, zif, cw, cb, wqk, bif, mg, L, NB):
    B, T, W = u.shape
    H, HD = ML_HEADS, ML_HEAD_DIM
    tok = lambda b, c: (b, c, 0)
    fixed2 = lambda b, c: (0, 0)
    outs = [
        jax.ShapeDtypeStruct((B, T, W), BF16),
        jax.ShapeDtypeStruct((B, H, HD, HD), F32),
        jax.ShapeDtypeStruct((B, SUBLANES, HD), F32),
        jax.ShapeDtypeStruct((B, SUBLANES, LANES), F32),
    ]
    return pl.pallas_call(
        functools.partial(_mlstm_prompt_kernel, L=L, NB=NB),
        grid=(B // NB, T // L),
        in_specs=[
            pl.BlockSpec((NB, L, W), tok),
            pl.BlockSpec((NB, L, W), tok),
            pl.BlockSpec((NB, L, W), tok),
            pl.BlockSpec((NB, L, LANES), tok),
            pl.BlockSpec((CONV_W, W), fixed2),
            pl.BlockSpec((1, W), fixed2),
            pl.BlockSpec((H, HD, 2 * HD), lambda b, c: (0, 0, 0)),
            pl.BlockSpec((1, LANES), fixed2),
            pl.BlockSpec((1, HD), fixed2),
        ],
        out_specs=[
            pl.BlockSpec((NB, L, W), tok),
            pl.BlockSpec((NB, H, HD, HD), lambda b, c: (b, 0, 0, 0)),
            pl.BlockSpec((NB, SUBLANES, HD), lambda b, c: (b, 0, 0)),
            pl.BlockSpec((NB, SUBLANES, LANES), lambda b, c: (b, 0, 0)),
        ],
        out_shape=outs,
        scratch_shapes=[
            pltpu.VMEM((NB, L + SUBLANES, W), F32),
            pltpu.VMEM((NB, H, HD, HD), F32),
            pltpu.VMEM((NB, SUBLANES, HD), F32),
            pltpu.VMEM((NB, SUBLANES, LANES), F32),
        ],
        compiler_params=_cparams(("parallel", "arbitrary")),
        name="mlstm_prompt",
    )(u, mv, mo, zif, cw, cb, wqk, bif, mg)


def _mlstm_sample_kernel(u_ref, c0_ref, c1_ref, c2_ref, mv_ref, mo_ref, zif_ref, cw_ref, cb_ref, wqk_ref,
                         bif_ref, mg_ref, C_in, n_in, m_in, hm_ref, C_out, n_out, m_out, *, TB):
    HD = ML_HEAD_DIM
    conv = (cb_ref[...] + c0_ref[...] * cw_ref[0:1, :] + c1_ref[...] * cw_ref[1:2, :]
            + c2_ref[...] * cw_ref[2:3, :] + u_ref[...] * cw_ref[3:4, :])
    c = (conv * jax.nn.sigmoid(conv)).astype(BF16)
    gact = _gate_activations(zif_ref[...], bif_ref[...])
    m0_all = m_in[...]
    rows = lax.broadcasted_iota(jnp.int32, (TB, HD), 0)
    m_new_all = jnp.zeros((TB, LANES), F32)
    lane = lax.broadcasted_iota(jnp.int32, (TB, LANES), 1)
    for h in range(ML_HEADS):
        sl = slice(h * HD, (h + 1) * HD)
        li = gact[:, h:h + 1]
        lf = gact[:, ML_HEADS + h:ML_HEADS + h + 1]
        m0 = m0_all[:, h:h + 1]
        inter = lf + m0
        m_t = jnp.maximum(inter, li)
        w_intra = jnp.exp(li - m_t)
        w_inter = jnp.exp(inter - m_t)
        qk = _dot(c[:, sl], wqk_ref[h])
        q = qk[:, 0:HD]
        k = qk[:, HD:2 * HD] * (HD ** -0.5)
        v = mv_ref[:, sl].astype(F32)
        n0 = n_in[:, sl]
        s = jnp.sum(q * k, axis=1, keepdims=True) * w_intra
        kw = k * w_intra
        vb = mv_ref[:, sl]
        qC = jnp.zeros((TB, HD), F32)
        for j in range(TB):
            Cj = C_in[j, h]
            only_j = rows == j
            qC = qC + _dot(jnp.where(only_j, q, 0.0).astype(BF16), Cj.astype(BF16))
            outer = _dot_tn(jnp.where(only_j, kw, 0.0).astype(BF16), vb)
            C_out[j, h] = w_inter[j:j + 1, :] * Cj + outer
        num = w_inter * qC + s * v
        den = w_inter * jnp.sum(q * n0, axis=1, keepdims=True) + s
        hval = num / jnp.maximum(jnp.abs(den), jnp.exp(-m_t))
        n_out[:, sl] = w_inter * n0 + kw
        m_new_all = jnp.where(lane == h, m_t, m_new_all)
        ms = jnp.mean(hval * hval, axis=-1, keepdims=True)
        y = hval * lax.rsqrt(ms + NORM_EPS) * mg_ref[...]
        hm_ref[:, sl] = (y * jax.nn.sigmoid(mo_ref[:, sl])).astype(BF16)
    m_out[...] = m_new_all


def _mlstm_sample(u, conv_state, mv, mo, zif, cw, cb, wqk, bif, mg, C0, n0, m0, TB):
    Bd, W = u.shape
    H, HD = ML_HEADS, ML_HEAD_DIM
    row = lambda i: (i, 0)
    fixed = lambda i: (0, 0)
    m0p = jnp.pad(m0, ((0, 0), (0, LANES - H)))
    outs = [
        jax.ShapeDtypeStruct((Bd, W), BF16),
        jax.ShapeDtypeStruct((Bd, H, HD, HD), F32),
        jax.ShapeDtypeStruct((Bd, W), F32),
        jax.ShapeDtypeStruct((Bd, LANES), F32),
    ]
    return pl.pallas_call(
        functools.partial(_mlstm_sample_kernel, TB=TB),
        grid=(Bd // TB,),
        in_specs=[
            pl.BlockSpec((TB, W), row),
            pl.BlockSpec((TB, W), row),
            pl.BlockSpec((TB, W), row),
            pl.BlockSpec((TB, W), row),
            pl.BlockSpec((TB, W), row),
            pl.BlockSpec((TB, W), row),
            pl.BlockSpec((TB, LANES), row),
            pl.BlockSpec((CONV_W, W), fixed),
            pl.BlockSpec((1, W), fixed),
            pl.BlockSpec((H, HD, 2 * HD), lambda i: (0, 0, 0)),
            pl.BlockSpec((1, LANES), fixed),
            pl.BlockSpec((1, HD), fixed),
            pl.BlockSpec((TB, H, HD, HD), lambda i: (i, 0, 0, 0)),
            pl.BlockSpec((TB, W), row),
            pl.BlockSpec((TB, LANES), row),
        ],
        out_specs=[
            pl.BlockSpec((TB, W), row),
            pl.BlockSpec((TB, H, HD, HD), lambda i: (i, 0, 0, 0)),
            pl.BlockSpec((TB, W), row),
            pl.BlockSpec((TB, LANES), row),
        ],
        out_shape=outs,
        compiler_params=_cparams(("parallel",)),
        name="mlstm_sample",
    )(u, conv_state[:, 0], conv_state[:, 1], conv_state[:, 2], mv, mo, zif, cw, cb, wqk, bif, mg,
      C0, n0.reshape(Bd, W), m0p)


def _merge_kernel(x_ref, oa_ref, hm_ref, g1_ref, wg_ref, bm_ref, wa_ref, wb_ref, wo_ref, o_ref):
    x = x_ref[...]
    D = x.shape[1]
    ms = jnp.mean(x * x, axis=-1, keepdims=True)
    h = (x * lax.rsqrt(ms + NORM_EPS) * g1_ref[...]).astype(BF16)
    g = jax.nn.sigmoid(_dot(h, wg_ref[...]) + bm_ref[...])
    ya = _dot(oa_ref[...], wa_ref[...])
    yb = _dot(hm_ref[...], wb_ref[...])
    mix = (g[:, 0:D] * ya + g[:, D:2 * D] * yb).astype(BF16)
    o_ref[...] = x + _dot(mix, wo_ref[...])


def _merge(x2, oa, hm, g1, wg, bm, wa, wb, wo, tm):
    M, D = x2.shape
    W = oa.shape[1]
    row = lambda i: (i, 0)
    fixed = lambda i: (0, 0)
    return pl.pallas_call(
        _merge_kernel,
        grid=(M // tm,),
        in_specs=[
            pl.BlockSpec((tm, D), row),
            pl.BlockSpec((tm, W), row),
            pl.BlockSpec((tm, W), row),
            pl.BlockSpec((1, D), fixed),
            pl.BlockSpec(wg.shape, fixed),
            pl.BlockSpec((1, 2 * D), fixed),
            pl.BlockSpec(wa.shape, fixed),
            pl.BlockSpec(wb.shape, fixed),
            pl.BlockSpec(wo.shape, fixed),
        ],
        out_specs=pl.BlockSpec((tm, D), row),
        out_shape=jax.ShapeDtypeStruct((M, D), F32),
        compiler_params=_cparams(("parallel",)),
        name="merge",
    )(x2, oa, hm, g1, wg, bm, wa, wb, wo)


def _ffn_kernel(x_ref, g2_ref, wgate_ref, wup_ref, wdown_ref, o_ref):
    x = x_ref[...]
    ms = jnp.mean(x * x, axis=-1, keepdims=True)
    h = (x * lax.rsqrt(ms + NORM_EPS) * g2_ref[...]).astype(BF16)
    a = _dot(h, wgate_ref[...])
    b = _dot(h, wup_ref[...])
    act = (a * jax.nn.sigmoid(a) * b).astype(BF16)
    o_ref[...] = x + _dot(act, wdown_ref[...])


def _ffn(x2, g2, wgate, wup, wdown, tm):
    M, D = x2.shape
    row = lambda i: (i, 0)
    fixed = lambda i: (0, 0)
    return pl.pallas_call(
        _ffn_kernel,
        grid=(M // tm,),
        in_specs=[
            pl.BlockSpec((tm, D), row),
            pl.BlockSpec((1, D), fixed),
            pl.BlockSpec(wgate.shape, fixed),
            pl.BlockSpec(wup.shape, fixed),
            pl.BlockSpec(wdown.shape, fixed),
        ],
        out_specs=pl.BlockSpec((tm, D), row),
        out_shape=jax.ShapeDtypeStruct((M, D), F32),
        compiler_params=_cparams(("parallel",)),
        name="ffn",
    )(x2, g2, wgate, wup, wdown)


def _rope_tables(pos):
    half = DA_HEAD_DIM // 2
    inv = ROPE_THETA ** (-jnp.arange(half, dtype=F32) / half)
    ang = pos.astype(F32)[:, None] * inv[None, :]
    cos = jnp.cos(ang)
    sin = jnp.sin(ang)
    cos = jnp.concatenate([cos, cos], axis=1)
    sin = jnp.concatenate([-sin, sin], axis=1)
    reps = LANES // DA_HEAD_DIM
    return jnp.concatenate([cos] * reps, axis=1), jnp.concatenate([sin] * reps, axis=1)


def _pick_tile(M, pref):
    t = min(pref, M)
    while M % t:
        t //= 2
    return t


def _layer_weights(w_in, qnorm_g, knorm_g, lambda_q1, lambda_k1, lambda_q2, lambda_k2, subln_g, w_a_out,
                   conv_w, conv_b, w_mq, w_mk, b_igate, b_fgate, mnorm_g, w_b_out, b_merge, w_o,
                   w_ffn_gate, w_ffn_up, w_ffn_down):
    W = DA_HEADS * DA_V_DIM
    n_main = 6 * W
    n_if = 2 * ML_HEADS
    w_if = jnp.pad(w_in[:, n_main:n_main + n_if], ((0, 0), (0, LANES - n_if)))
    p = {}
    p['w_main'] = jnp.concatenate([w_in[:, :n_main], w_if], axis=1).astype(BF16)
    p['w_gates'] = w_in[:, n_main + n_if:].astype(BF16)
    p['qg'] = jnp.tile(qnorm_g, 2 * DA_HEADS)[None, :]
    p['kg'] = jnp.tile(knorm_g, 2 * DA_HEADS)[None, :]
    grp = np.arange(W) // DA_HEAD_DIM
    p['gm'] = jnp.asarray((grp[:, None] == grp[None, :]).astype(np.float32) / DA_HEAD_DIM, dtype=BF16)
    p['lams'] = jnp.stack([lambda_q1, lambda_k1, lambda_q2, lambda_k2]).astype(F32)
    p['sg'] = subln_g[None, :]
    p['w_a_out'] = w_a_out.astype(BF16)
    p['conv_w'] = conv_w
    p['conv_b'] = conv_b[None, :]
    p['wqk'] = jnp.concatenate([w_mq, w_mk], axis=2).astype(BF16)
    p['bif'] = jnp.pad(jnp.concatenate([b_igate, b_fgate]), (0, LANES - n_if))[None, :]
    p['mg'] = mnorm_g[None, :]
    p['w_b_out'] = w_b_out.astype(BF16)
    p['b_merge'] = b_merge[None, :]
    p['w_o'] = w_o.astype(BF16)
    p['w_ffn_gate'] = w_ffn_gate.astype(BF16)
    p['w_ffn_up'] = w_ffn_up.astype(BF16)
    p['w_ffn_down'] = w_ffn_down.astype(BF16)
    return p


def kernel(x_prompt, x_sample, cache_k, cache_v, page_table, state_C, state_n, state_m, state_conv, norm1_g, w_in, qnorm_g, knorm_g, lambda_q1, lambda_k1, lambda_q2, lambda_k2, subln_g, w_a_out, conv_w, conv_b, w_mq, w_mk, b_igate, b_fgate, mnorm_g, w_b_out, b_merge, w_o, norm2_g, w_ffn_gate, w_ffn_up, w_ffn_down):
    Bp, Tp, D = x_prompt.shape
    Bs, Ts, _ = x_sample.shape
    assert Ts == 1, "the sample group decodes one token per sequence"
    depth = w_in.shape[0]
    n_pages = page_table.shape[1]
    page_rows = cache_k.shape[2]
    past_len = n_pages * page_rows
    W = DA_HEADS * DA_V_DIM
    H, HD = ML_HEADS, ML_HEAD_DIM

    Mp = Bp * Tp
    tm_p = _pick_tile(Tp, 512)
    tm_s = _pick_tile(Bs, 128)
    cos_p, sin_p = _rope_tables(jnp.arange(Tp))
    cos_s, sin_s = _rope_tables(past_len + jnp.arange(Ts))
    cos_s = jnp.tile(cos_s, (tm_s, 1))
    sin_s = jnp.tile(sin_s, (tm_s, 1))
    tq = _pick_tile(Tp, 1024)
    rq = min(tq, 256)
    L = _pick_tile(Tp, 256)
    NB = 1
    G = _pick_tile(n_pages, 32)
    TB = _pick_tile(Bs, 8)

    yp = x_prompt.reshape(Mp, D)
    ys = x_sample.reshape(Bs, D)
    outs = {k: [] for k in ('kp', 'vp', 'Cp', 'np', 'mp', 'cp', 'ks', 'vs', 'Cs', 'ns', 'ms', 'cs')}
    for l in range(depth):
        p = _layer_weights(w_in[l], qnorm_g[l], knorm_g[l], lambda_q1[l], lambda_k1[l], lambda_q2[l],
                           lambda_k2[l], subln_g[l], w_a_out[l], conv_w[l], conv_b[l], w_mq[l], w_mk[l],
                           b_igate[l], b_fgate[l], mnorm_g[l], w_b_out[l], b_merge[l], w_o[l],
                           w_ffn_gate[l], w_ffn_up[l], w_ffn_down[l])
        g1 = norm1_g[l][None, :]
        g2 = norm2_g[l][None, :]
        lam_init = 0.8 - 0.6 * math.exp(-0.3 * l)

        q, k, kb, v, vb, u, mv, mo, zif = _in_proj(yp, g1, p['w_main'], p['qg'], p['kg'], cos_p, sin_p,
                                                   p['gm'], tm_p, BF16, True)
        oa = _attn_prompt(q.reshape(Bp, Tp, W), kb.reshape(Bp, Tp, W), vb.reshape(Bp, Tp, W),
                          p['lams'], p['sg'], lam_init, tq, rq)
        hm, Cp, np_, mp = _mlstm_prompt(u.reshape(Bp, Tp, W), mv.reshape(Bp, Tp, W), mo.reshape(Bp, Tp, W),
                                        zif.reshape(Bp, Tp, LANES), p['conv_w'], p['conv_b'], p['wqk'],
                                        p['bif'], p['mg'], L, NB)
        x1 = _merge(yp, oa.reshape(Mp, W), hm.reshape(Mp, W), g1, p['w_gates'], p['b_merge'],
                    p['w_a_out'], p['w_b_out'], p['w_o'], tm_p)
        yp = _ffn(x1, g2, p['w_ffn_gate'], p['w_ffn_up'], p['w_ffn_down'], tm_p)
        outs['kp'].append(jnp.transpose(k.reshape(Bp, 2 * DA_HEADS, DA_HEAD_DIM, Tp), (0, 3, 1, 2)))
        outs['vp'].append(v.reshape(Bp, Tp, DA_HEADS, DA_V_DIM))
        outs['Cp'].append(Cp)
        outs['np'].append(np_[:, :H, :])
        outs['mp'].append(mp[:, :H, 0])
        outs['cp'].append(u.reshape(Bp, Tp, W)[:, Tp - (CONV_W - 1):, :])

        q, k, kb, v, vb, u, mv, mo, zif = _in_proj(ys, g1, p['w_main'], p['qg'], p['kg'], cos_s, sin_s,
                                                   p['gm'], tm_s, F32, False)
        n_phys = cache_k.shape[1]
        ckt = jnp.transpose(cache_k[l], (0, 2, 3, 1)).reshape(n_phys, W, page_rows)
        cv2 = cache_v[l].reshape(n_phys, page_rows * DA_HEADS, DA_V_DIM)
        oa = _attn_sample(q, k, v, ckt, cv2, page_table, p['lams'], p['sg'], lam_init, G)
        hm, Cs, ns, ms = _mlstm_sample(u, state_conv[l], mv, mo, zif, p['conv_w'], p['conv_b'], p['wqk'],
                                       p['bif'], p['mg'], state_C[l], state_n[l], state_m[l], TB)
        x1 = _merge(ys, oa, hm, g1, p['w_gates'], p['b_merge'], p['w_a_out'], p['w_b_out'], p['w_o'], tm_s)
        ys = _ffn(x1, g2, p['w_ffn_gate'], p['w_ffn_up'], p['w_ffn_down'], tm_s)
        outs['ks'].append(k.reshape(Bs, Ts, 2 * DA_HEADS, DA_HEAD_DIM))
        outs['vs'].append(v.reshape(Bs, Ts, DA_HEADS, DA_V_DIM))
        outs['Cs'].append(Cs)
        outs['ns'].append(ns.reshape(Bs, H, HD))
        outs['ms'].append(ms[:, :H])
        outs['cs'].append(jnp.concatenate([state_conv[l][:, 1:, :], u[:, None, :]], axis=1))

    st = lambda name: jnp.stack(outs[name])
    return (yp.reshape(Bp, Tp, D), ys.reshape(Bs, Ts, D),
            st('kp'), st('vp'), st('Cp'), st('np'), st('mp'), st('cp'),
            st('ks'), st('vs'), st('Cs'), st('ns'), st('ms'), st('cs'))
```

```python
import functools
import math

import jax
import jax.numpy as jnp
import numpy as np
from jax import lax
from jax.experimental import pallas as pl
from jax.experimental.pallas import tpu as pltpu

F32 = jnp.float32
BF16 = jnp.bfloat16

DA_HEADS = 4
DA_HEAD_DIM = 64
DA_V_DIM = 2 * DA_HEAD_DIM
ML_HEADS = 4
ML_HEAD_DIM = 128
CONV_W = 4
ROPE_THETA = 10000.0
NORM_EPS = 1e-6

LANES = 128
SUBLANES = 8
VMEM_LIMIT = 56 * 1024 * 1024

_HIGHEST = lax.Precision.HIGHEST


def _cparams(sem):
    return pltpu.CompilerParams(dimension_semantics=sem, vmem_limit_bytes=VMEM_LIMIT)


def _dot(a, b):
    return jnp.dot(a, b, preferred_element_type=F32)


def _dot_nt(a, b):
    return lax.dot_general(a, b, (((1,), (1,)), ((), ())), preferred_element_type=F32)


def _dot_tn(a, b):
    return lax.dot_general(a, b, (((0,), (0,)), ((), ())), preferred_element_type=F32)


def _log_sigmoid(x):
    return jnp.minimum(x, 0.0) - jnp.log(1.0 + jnp.exp(-jnp.abs(x)))


def _causal_conv_silu(u, ext, cw_ref, cb_ref):
    n = u.shape[0]
    ext[SUBLANES:SUBLANES + n, :] = u
    conv = cb_ref[...]
    for j in range(CONV_W):
        off = SUBLANES - (CONV_W - 1) + j
        conv = conv + ext[off:off + n, :] * cw_ref[j:j + 1, :]
    ext[0:SUBLANES, :] = u[n - SUBLANES:n, :]
    return conv * jax.nn.sigmoid(conv)


def _in_proj_kernel(x_ref, g1_ref, w_ref, qg_ref, kg_ref, cos_ref, sin_ref, gm_ref,
                    q_ref, k_ref, kb_ref, v_ref, vb_ref, u_ref, mv_ref, mo_ref, zif_ref, *, cache_layout):
    x = x_ref[...]
    ms = jnp.mean(x * x, axis=-1, keepdims=True)
    h = (x * lax.rsqrt(ms + NORM_EPS) * g1_ref[...]).astype(BF16)
    z = _dot(h, w_ref[...])
    width = DA_HEADS * DA_V_DIM
    cos = jnp.concatenate([cos_ref[...]] * (width // LANES), axis=1)
    sin = jnp.concatenate([sin_ref[...]] * (width // LANES), axis=1)
    lane = lax.broadcasted_iota(jnp.int32, (x.shape[0], width), 1)
    first_half = (lane % DA_HEAD_DIM) < (DA_HEAD_DIM // 2)

    def qk_norm_rope(t, g):
        ms_g = _dot((t * t).astype(BF16), gm_ref[...])
        y = t * lax.rsqrt(ms_g + NORM_EPS) * g
        partner = jnp.where(first_half,
                            pltpu.roll(y, width - DA_HEAD_DIM // 2, 1),
                            pltpu.roll(y, DA_HEAD_DIM // 2, 1))
        return y * cos + partner * sin

    q = qk_norm_rope(z[:, 0:width], qg_ref[...])
    k = qk_norm_rope(z[:, width:2 * width], kg_ref[...])
    q_ref[...] = (q * (DA_HEAD_DIM ** -0.5)).astype(q_ref.dtype)
    v = z[:, 2 * width:3 * width]
    if cache_layout:
        tm = x.shape[0]
        for j in range(width // LANES):
            k_ref[j * LANES:(j + 1) * LANES, :] = jnp.transpose(k[:, j * LANES:(j + 1) * LANES])
        for hd in range(DA_HEADS):
            v_ref[pl.ds(hd, tm, stride=DA_HEADS), :] = v[:, hd * DA_V_DIM:(hd + 1) * DA_V_DIM]
    else:
        k_ref[...] = k
        v_ref[...] = v
    kb_ref[...] = k.astype(BF16)
    vb_ref[...] = v.astype(BF16)
    u_ref[...] = z[:, 3 * width:4 * width]
    mv_ref[...] = z[:, 4 * width:5 * width].astype(BF16)
    mo_ref[...] = z[:, 5 * width:6 * width]
    zif_ref[...] = z[:, 6 * width:6 * width + LANES]


def _in_proj(x2, g1, w, qg, kg, cos, sin, gm, tm, q_dtype, cache_layout):
    M, D = x2.shape
    W = 512
    tab_blocks = cos.shape[0] // tm
    row = lambda i: (i, 0)
    tab = lambda i: (i % tab_blocks, 0)
    fixed = lambda i: (0, 0)
    if cache_layout:
        R = cos.shape[0]
        k_out = jax.ShapeDtypeStruct((M // R, W, R), F32)
        k_spec = pl.BlockSpec((None, W, tm), lambda i: (i // tab_blocks, 0, i % tab_blocks))
        v_out = jax.ShapeDtypeStruct((M * DA_HEADS, DA_V_DIM), F32)
        v_spec = pl.BlockSpec((tm * DA_HEADS, DA_V_DIM), row)
    else:
        k_out = v_out = jax.ShapeDtypeStruct((M, W), F32)
        k_spec = v_spec = pl.BlockSpec((tm, W), row)
    outs = [
        jax.ShapeDtypeStruct((M, W), q_dtype),
        k_out,
        jax.ShapeDtypeStruct((M, W), BF16),
        v_out,
        jax.ShapeDtypeStruct((M, W), BF16),
        jax.ShapeDtypeStruct((M, W), F32),
        jax.ShapeDtypeStruct((M, W), BF16),
        jax.ShapeDtypeStruct((M, W), F32),
        jax.ShapeDtypeStruct((M, LANES), F32),
    ]
    out_specs = [pl.BlockSpec((tm, o.shape[1]), row) for o in outs]
    out_specs[1] = k_spec
    out_specs[3] = v_spec
    return pl.pallas_call(
        functools.partial(_in_proj_kernel, cache_layout=cache_layout),
        grid=(M // tm,),
        in_specs=[
            pl.BlockSpec((tm, D), row),
            pl.BlockSpec((1, D), fixed),
            pl.BlockSpec(w.shape, fixed),
            pl.BlockSpec((1, W), fixed),
            pl.BlockSpec((1, W), fixed),
            pl.BlockSpec((tm, LANES), tab),
            pl.BlockSpec((tm, LANES), tab),
            pl.BlockSpec((W, W), fixed),
        ],
        out_specs=out_specs,
        out_shape=outs,
        compiler_params=_cparams(("parallel",)),
        name="in_proj",
    )(x2, g1, w, qg, kg, cos, sin, gm)


def _lambda_value(lams, lam_init):
    a = jnp.sum(lams[0:1, :] * lams[1:2, :], axis=1, keepdims=True)
    b = jnp.sum(lams[2:3, :] * lams[3:4, :], axis=1, keepdims=True)
    return jnp.exp(a) - jnp.exp(b) + lam_init


def _lane_tile(x, n):
    return x if n == 1 else jnp.concatenate([x] * n, axis=1)


def _attn_prompt_kernel(qi_tab, ki_tab, lams_ref, q_ref, k_ref, v_ref, sg_ref, o_ref,
                        qs, vx, m_s, acc_s, *, tq, rq_full, rq_diag, lam_init):
    step = pl.program_id(2)
    qi = qi_tab[step]
    ki = ki_tab[step]
    tk = tq

    @pl.when(ki == 0)
    def _():
        q = q_ref[...].astype(F32)
        lane = lax.broadcasted_iota(jnp.int32, q.shape, 1)
        qs[0:tq, :] = jnp.where(lane < DA_HEAD_DIM, q, 0.0).astype(BF16)
        qs[tq:2 * tq, :] = jnp.where(lane >= DA_HEAD_DIM, q, 0.0).astype(BF16)
        vx[:, LANES:2 * LANES] = jnp.ones((tk, LANES), BF16)
        m_s[...] = jnp.full(m_s.shape, -jnp.inf, F32)
        acc_s[...] = jnp.zeros(acc_s.shape, F32)

    def update(masked):
        rq = rq_diag if masked else rq_full
        vx[:, 0:LANES] = v_ref[...]
        for r0 in range(0, 2 * tq, rq):
            off = r0 % tq
            nk = min(tk, off + rq) if masked else tk
            rows = slice(r0, r0 + rq)
            s = _dot_nt(qs[rows, :], k_ref[0:nk, :])
            if masked:
                row = lax.broadcasted_iota(jnp.int32, (rq, rq), 0)
                col = lax.broadcasted_iota(jnp.int32, (rq, rq), 1)
                diag = jnp.where(col <= row, s[:, off:off + rq], -jnp.inf)
                s = diag if off == 0 else jnp.concatenate([s[:, 0:off], diag], axis=1)
            m_prev = m_s[rows, :]
            m_new = jnp.maximum(m_prev, jnp.max(s, axis=1, keepdims=True))
            alpha = jnp.exp(m_prev - m_new)
            p = jnp.exp(s - _lane_tile(m_new, nk // LANES))
            pv = _dot(p.astype(BF16), vx[0:nk, :])
            acc_s[rows, :] = _lane_tile(alpha, 2) * acc_s[rows, :] + pv
            m_s[rows, :] = m_new

    @pl.when(ki < qi)
    def _():
        update(False)

    @pl.when(ki == qi)
    def _():
        update(True)
        o = acc_s[:, 0:LANES] / acc_s[:, LANES:2 * LANES]
        lam = _lambda_value(lams_ref[...], lam_init)
        d = o[0:tq, :] - lam * o[tq:2 * tq, :]
        ms = jnp.mean(d * d, axis=-1, keepdims=True)
        y = d * lax.rsqrt(ms + NORM_EPS) * sg_ref[...] * (1.0 - lam_init)
        o_ref[...] = y.astype(BF16)


def _attn_prompt(q, k, v, lams, sg, lam_init, tq, rq_full, rq_diag):
    B, T, W = q.shape
    nq = T // tq
    qi_tab = np.concatenate([np.full(i + 1, i, np.int32) for i in range(nq)])
    ki_tab = np.concatenate([np.arange(i + 1, dtype=np.int32) for i in range(nq)])
    nsteps = int(qi_tab.shape[0])
    grid_spec = pltpu.PrefetchScalarGridSpec(
        num_scalar_prefetch=2,
        grid=(B, DA_HEADS, nsteps),
        in_specs=[
            pl.BlockSpec((4, DA_HEAD_DIM), lambda b, h, s, qt, kt: (0, 0)),
            pl.BlockSpec((None, tq, LANES), lambda b, h, s, qt, kt: (b, qt[s], h)),
            pl.BlockSpec((None, tq, LANES), lambda b, h, s, qt, kt: (b, kt[s], h)),
            pl.BlockSpec((None, tq, LANES), lambda b, h, s, qt, kt: (b, kt[s], h)),
            pl.BlockSpec((1, LANES), lambda b, h, s, qt, kt: (0, 0)),
        ],
        out_specs=pl.BlockSpec((None, tq, LANES), lambda b, h, s, qt, kt: (b, qt[s], h)),
        scratch_shapes=[
            pltpu.VMEM((2 * tq, LANES), BF16),
            pltpu.VMEM((tq, 2 * LANES), BF16),
            pltpu.VMEM((2 * tq, LANES), F32),
            pltpu.VMEM((2 * tq, 2 * LANES), F32),
        ],
    )
    return pl.pallas_call(
        functools.partial(_attn_prompt_kernel, tq=tq, rq_full=rq_full, rq_diag=rq_diag, lam_init=lam_init),
        grid_spec=grid_spec,
        out_shape=jax.ShapeDtypeStruct((B, T, W), BF16),
        compiler_params=_cparams(("parallel", "parallel", "arbitrary")),
        name="attn_prompt",
    )(jnp.asarray(qi_tab), jnp.asarray(ki_tab), lams, q, k, v, sg)


def _attn_sample_kernel(pt_ref, lams_ref, q_ref, kn_ref, vn_ref, sg_ref, *rest, G, lam_init):
    k_refs = rest[0:G]
    v_refs = rest[G:2 * G]
    o_ref = rest[2 * G]
    qcol, m_s, l_s, acc = rest[2 * G + 1:]
    step = pl.program_id(1)
    W = qcol.shape[0]
    n_maps = 2 * DA_HEADS
    n_lane_blocks = W // LANES

    @pl.when(step == 0)
    def _():
        qb = jnp.broadcast_to(q_ref[...], (LANES, W))
        for j in range(n_lane_blocks):
            qcol[j * LANES:(j + 1) * LANES, :] = jnp.transpose(qb[:, j * LANES:(j + 1) * LANES])
        m_s[...] = jnp.full(m_s.shape, -jnp.inf, F32)
        l_s[...] = jnp.zeros(l_s.shape, F32)
        acc[...] = jnp.zeros(acc.shape, F32)

    qc = qcol[...]
    s_list = []
    for i in range(G):
        prod = k_refs[i][...] * qc
        s_list.append(jnp.sum(prod.reshape(n_maps, DA_HEAD_DIM, LANES), axis=1))
    m_cur = functools.reduce(jnp.maximum, s_list)
    m_prev = m_s[...]
    m_new = jnp.maximum(m_prev, jnp.max(m_cur, axis=1, keepdims=True))
    alpha = jnp.exp(m_prev - m_new)
    p_list = [jnp.exp(s - m_new) for s in s_list]
    l_s[...] = alpha * l_s[...] + functools.reduce(jnp.add, p_list)
    p_all = jnp.concatenate(p_list, axis=1).astype(BF16)
    v_all = jnp.concatenate(
        [jnp.concatenate([v_refs[i][pl.ds(h, LANES, stride=DA_HEADS), :] for h in range(DA_HEADS)], axis=1)
         for i in range(G)], axis=0).astype(BF16)
    acc[...] = _lane_tile(alpha, n_lane_blocks) * acc[...] + _dot(p_all, v_all)
    m_s[...] = m_new

    @pl.when(step == pl.num_programs(1) - 1)
    def _():
        q8 = jnp.broadcast_to(q_ref[...], (n_maps, W))
        r = lax.broadcasted_iota(jnp.int32, (n_maps, W), 0)
        c = lax.broadcasted_iota(jnp.int32, (n_maps, W), 1)
        qm = jnp.where((c // DA_HEAD_DIM) == r, q8, 0.0)
        s_new = jnp.sum(qm * kn_ref[...], axis=1, keepdims=True)
        m_last = m_s[...]
        m_fin = jnp.maximum(m_last, s_new)
        a_fin = jnp.exp(m_last - m_fin)
        p_new = jnp.exp(s_new - m_fin)
        l_tot = jnp.sum(a_fin * l_s[...], axis=1, keepdims=True) + p_new[:, 0:1]
        o8 = (_lane_tile(a_fin, n_lane_blocks) * acc[...]
              + _lane_tile(p_new, n_lane_blocks) * vn_ref[...]) / l_tot
        lam = _lambda_value(lams_ref[...], lam_init)
        parts = []
        for h in range(DA_HEADS):
            blk = o8[:, h * DA_V_DIM:(h + 1) * DA_V_DIM]
            d = blk[2 * h:2 * h + 1, :] - lam * blk[2 * h + 1:2 * h + 2, :]
            ms = jnp.mean(d * d, axis=-1, keepdims=True)
            parts.append(d * lax.rsqrt(ms + NORM_EPS) * sg_ref[...] * (1.0 - lam_init))
        o_ref[...] = jnp.concatenate(parts, axis=1).astype(o_ref.dtype)


def _attn_sample(q, k_new, v_new, cache_kt, cache_v2, page_table, lams, sg, lam_init, G):
    Bd, W = q.shape
    NP = page_table.shape[1]
    assert cache_kt.shape[1:] == (W, LANES) and cache_v2.shape[1:] == (W, LANES)
    pt = page_table.reshape(-1)

    def page_map(i):
        return lambda b, s, pt: (pt[b * NP + s * G + i], 0, 0)

    per_seq = lambda b, s, pt: (b, 0, 0)
    fixed = lambda b, s, pt: (0, 0)
    grid_spec = pltpu.PrefetchScalarGridSpec(
        num_scalar_prefetch=1,
        grid=(Bd, NP // G),
        in_specs=[
            pl.BlockSpec((4, DA_HEAD_DIM), fixed),
            pl.BlockSpec((None, 1, W), per_seq),
            pl.BlockSpec((None, 1, W), per_seq),
            pl.BlockSpec((None, 1, W), per_seq),
            pl.BlockSpec((1, LANES), fixed),
        ] + [pl.BlockSpec((None, W, LANES), page_map(i)) for i in range(G)]
          + [pl.BlockSpec((None, W, LANES), page_map(i)) for i in range(G)],
        out_specs=pl.BlockSpec((None, 1, W), per_seq),
        scratch_shapes=[
            pltpu.VMEM((W, LANES), F32),
            pltpu.VMEM((2 * DA_HEADS, LANES), F32),
            pltpu.VMEM((2 * DA_HEADS, LANES), F32),
            pltpu.VMEM((2 * DA_HEADS, W), F32),
        ],
    )
    out = pl.pallas_call(
        functools.partial(_attn_sample_kernel, G=G, lam_init=lam_init),
        grid_spec=grid_spec,
        out_shape=jax.ShapeDtypeStruct((Bd, 1, W), BF16),
        compiler_params=_cparams(("parallel", "arbitrary")),
        name="attn_sample",
    )(pt, lams, q.reshape(Bd, 1, W), k_new.reshape(Bd, 1, W), v_new.reshape(Bd, 1, W), sg,
      *([cache_kt] * G), *([cache_v2] * G))
    return out.reshape(Bd, W)


def _gate_activations(zif, bif):
    g = zif + bif
    lane = lax.broadcasted_iota(jnp.int32, g.shape, 1)
    return jnp.where(lane < ML_HEADS, g, _log_sigmoid(g))


def _mlstm_prompt_kernel(u_ref, mv_ref, mo_ref, zif_ref, cw_ref, cb_ref, wqk_ref, bif_ref, mg_ref,
                         hm_ref, C_ref, n_ref, m_ref, ext, CN_s, m_s, *, L):
    chunk = pl.program_id(1)
    HD = ML_HEAD_DIM
    nl = L // LANES

    @pl.when(chunk == 0)
    def _():
        ext[0:SUBLANES, :] = jnp.zeros((SUBLANES, ext.shape[1]), F32)
        CN_s[...] = jnp.zeros(CN_s.shape, F32)
        m_s[...] = jnp.zeros(m_s.shape, F32)

    c = _causal_conv_silu(u_ref[...], ext, cw_ref, cb_ref).astype(BF16)
    ones = jnp.ones((L, HD), BF16)

    gact = _gate_activations(zif_ref[...], bif_ref[...])
    gact_t = jnp.transpose(gact)
    row = lax.broadcasted_iota(jnp.int32, (L, L), 0)
    col = lax.broadcasted_iota(jnp.int32, (L, L), 1)
    causal = col <= row
    tri = causal.astype(F32)
    cum_col = jnp.dot(tri, gact, precision=_HIGHEST, preferred_element_type=F32)
    cum_row = lax.dot_general(gact_t[0:SUBLANES, :], tri, (((1,), (1,)), ((), ())),
                              precision=_HIGHEST, preferred_element_type=F32)

    for h in range(ML_HEADS):
        sl = slice(h * HD, (h + 1) * HD)
        b_rep = jnp.broadcast_to(cum_col[:, ML_HEADS + h:ML_HEADS + h + 1], (L, LANES))
        li_rep = jnp.broadcast_to(gact[:, h:h + 1], (L, LANES))
        b_row = cum_row[ML_HEADS + h:ML_HEADS + h + 1, :]
        li_row = gact_t[h:h + 1, :]
        m_prev = m_s[h:h + 1, :]
        dmat = jnp.where(causal, _lane_tile(b_rep, nl) - b_row + li_row, -jnp.inf)
        inter = b_rep + m_prev
        m_t = jnp.maximum(inter, jnp.max(dmat, axis=1, keepdims=True))
        w_intra = jnp.exp(dmat - _lane_tile(m_t, nl))
        w_inter = jnp.exp(inter - m_t)
        qk = _dot(c[:, sl], wqk_ref[h])
        q = qk[:, 0:HD]
        k = qk[:, HD:2 * HD] * (HD ** -0.5)
        qb = q.astype(BF16)
        vx = jnp.concatenate([mv_ref[:, sl], ones], axis=1)
        s = _dot_nt(qb, k.astype(BF16)) * w_intra
        CN = CN_s[h]
        nd = _lane_tile(w_inter, 2) * _dot(qb, CN.astype(BF16)) + _dot(s.astype(BF16), vx)
        hval = nd[:, 0:HD] / jnp.maximum(jnp.abs(nd[:, HD:2 * HD]), jnp.exp(-m_t))
        m_new = m_t[L - 1:L, :]
        b_last = b_rep[L - 1:L, :]
        w_end = jnp.exp(b_last - b_rep + li_rep - m_new)
        decay = jnp.exp(b_last + m_prev - m_new)
        kw = k * w_end
        CN_s[h] = _lane_tile(decay, 2) * CN + _dot_tn(kw.astype(BF16), vx)
        m_s[h:h + 1, :] = m_new
        ms = jnp.mean(hval * hval, axis=-1, keepdims=True)
        y = hval * lax.rsqrt(ms + NORM_EPS) * mg_ref[...]
        hm_ref[:, sl] = (y * jax.nn.sigmoid(mo_ref[:, sl])).astype(BF16)

    @pl.when(chunk == pl.num_programs(1) - 1)
    def _():
        for h in range(ML_HEADS):
            CN = CN_s[h]
            C_ref[h] = CN[:, 0:HD]
            n_ref[h:h + 1, :] = jnp.transpose(CN[:, HD:2 * HD])[0:1, :]
        n_ref[ML_HEADS:SUBLANES, :] = jnp.zeros((SUBLANES - ML_HEADS, HD), F32)
        m_ref[...] = m_s[...]


def _mlstm_prompt(u, mv, mo, zif, cw, cb, wqk, bif, mg, L):
    B, T, W = u.shape
    H, HD = ML_HEADS, ML_HEAD_DIM
    tok = lambda b, j: (b, j, 0)
    fixed2 = lambda b, j: (0, 0)
    outs = [
        jax.ShapeDtypeStruct((B, T, W), BF16),
        jax.ShapeDtypeStruct((B, H, HD, HD), F32),
        jax.ShapeDtypeStruct((B, SUBLANES, HD), F32),
        jax.ShapeDtypeStruct((B, SUBLANES, LANES), F32),
    ]
    return pl.pallas_call(
        functools.partial(_mlstm_prompt_kernel, L=L),
        grid=(B, T // L),
        in_specs=[
            pl.BlockSpec((None, L, W), tok),
            pl.BlockSpec((None, L, W), tok),
            pl.BlockSpec((None, L, W), tok),
            pl.BlockSpec((None, L, LANES), tok),
            pl.BlockSpec((CONV_W, W), fixed2),
            pl.BlockSpec((1, W), fixed2),
            pl.BlockSpec((H, HD, 2 * HD), lambda b, j: (0, 0, 0)),
            pl.BlockSpec((1, LANES), fixed2),
            pl.BlockSpec((1, HD), fixed2),
        ],
        out_specs=[
            pl.BlockSpec((None, L, W), tok),
            pl.BlockSpec((None, H, HD, HD), lambda b, j: (b, 0, 0, 0)),
            pl.BlockSpec((None, SUBLANES, HD), lambda b, j: (b, 0, 0)),
            pl.BlockSpec((None, SUBLANES, LANES), lambda b, j: (b, 0, 0)),
        ],
        out_shape=outs,
        scratch_shapes=[
            pltpu.VMEM((L + SUBLANES, W), F32),
            pltpu.VMEM((H, HD, 2 * HD), F32),
            pltpu.VMEM((SUBLANES, LANES), F32),
        ],
        compiler_params=_cparams(("parallel", "arbitrary")),
        name="mlstm_prompt",
    )(u, mv, mo, zif, cw, cb, wqk, bif, mg)


def _mlstm_sample_kernel(u_ref, c0_ref, c1_ref, c2_ref, mv_ref, mo_ref, zif_ref, cw_ref, cb_ref, wqk_ref,
                         bif_ref, mg_ref, C_in, n_in, m_in, hm_ref, C_out, n_out, m_out, *, TB):
    HD = ML_HEAD_DIM
    conv = (cb_ref[...] + c0_ref[...] * cw_ref[0:1, :] + c1_ref[...] * cw_ref[1:2, :]
            + c2_ref[...] * cw_ref[2:3, :] + u_ref[...] * cw_ref[3:4, :])
    c = (conv * jax.nn.sigmoid(conv)).astype(BF16)
    gact = _gate_activations(zif_ref[...], bif_ref[...])
    m0_all = m_in[...]
    rows = lax.broadcasted_iota(jnp.int32, (TB, HD), 0)
    m_new_all = jnp.zeros((TB, LANES), F32)
    lane = lax.broadcasted_iota(jnp.int32, (TB, LANES), 1)
    for h in range(ML_HEADS):
        sl = slice(h * HD, (h + 1) * HD)
        li = gact[:, h:h + 1]
        lf = gact[:, ML_HEADS + h:ML_HEADS + h + 1]
        m0 = m0_all[:, h:h + 1]
        inter = lf + m0
        m_t = jnp.maximum(inter, li)
        w_intra = jnp.exp(li - m_t)
        w_inter = jnp.exp(inter - m_t)
        qk = _dot(c[:, sl], wqk_ref[h])
        q = qk[:, 0:HD]
        k = qk[:, HD:2 * HD] * (HD ** -0.5)
        v = mv_ref[:, sl].astype(F32)
        n0 = n_in[:, sl]
        s = jnp.sum(q * k, axis=1, keepdims=True) * w_intra
        kw = k * w_intra
        vb = mv_ref[:, sl]
        qC = jnp.zeros((TB, HD), F32)
        for j in range(TB):
            Cj = C_in[j, h]
            only_j = rows == j
            qC = qC + _dot(jnp.where(only_j, q, 0.0).astype(BF16), Cj.astype(BF16))
            outer = _dot_tn(jnp.where(only_j, kw, 0.0).astype(BF16), vb)
            C_out[j, h] = w_inter[j:j + 1, :] * Cj + outer
        num = w_inter * qC + s * v
        den = w_inter * jnp.sum(q * n0, axis=1, keepdims=True) + s
        hval = num / jnp.maximum(jnp.abs(den), jnp.exp(-m_t))
        n_out[:, sl] = w_inter * n0 + kw
        m_new_all = jnp.where(lane == h, m_t, m_new_all)
        ms = jnp.mean(hval * hval, axis=-1, keepdims=True)
        y = hval * lax.rsqrt(ms + NORM_EPS) * mg_ref[...]
        hm_ref[:, sl] = (y * jax.nn.sigmoid(mo_ref[:, sl])).astype(BF16)
    m_out[...] = m_new_all


def _mlstm_sample(u, conv_state, mv, mo, zif, cw, cb, wqk, bif, mg, C0, n0, m0, TB):
    Bd, W = u.shape
    H, HD = ML_HEADS, ML_HEAD_DIM
    row = lambda i: (i, 0)
    fixed = lambda i: (0, 0)
    m0p = jnp.pad(m0, ((0, 0), (0, LANES - H)))
    outs = [
        jax.ShapeDtypeStruct((Bd, W), BF16),
        jax.ShapeDtypeStruct((Bd, H, HD, HD), F32),
        jax.ShapeDtypeStruct((Bd, W), F32),
        jax.ShapeDtypeStruct((Bd, LANES), F32),
    ]
    return pl.pallas_call(
        functools.partial(_mlstm_sample_kernel, TB=TB),
        grid=(Bd // TB,),
        in_specs=[
            pl.BlockSpec((TB, W), row),
            pl.BlockSpec((TB, W), row),
            pl.BlockSpec((TB, W), row),
            pl.BlockSpec((TB, W), row),
            pl.BlockSpec((TB, W), row),
            pl.BlockSpec((TB, W), row),
            pl.BlockSpec((TB, LANES), row),
            pl.BlockSpec((CONV_W, W), fixed),
            pl.BlockSpec((1, W), fixed),
            pl.BlockSpec((H, HD, 2 * HD), lambda i: (0, 0, 0)),
            pl.BlockSpec((1, LANES), fixed),
            pl.BlockSpec((1, HD), fixed),
            pl.BlockSpec((TB, H, HD, HD), lambda i: (i, 0, 0, 0)),
            pl.BlockSpec((TB, W), row),
            pl.BlockSpec((TB, LANES), row),
        ],
        out_specs=[
            pl.BlockSpec((TB, W), row),
            pl.BlockSpec((TB, H, HD, HD), lambda i: (i, 0, 0, 0)),
            pl.BlockSpec((TB, W), row),
            pl.BlockSpec((TB, LANES), row),
        ],
        out_shape=outs,
        compiler_params=_cparams(("parallel",)),
        name="mlstm_sample",
    )(u, conv_state[:, 0], conv_state[:, 1], conv_state[:, 2], mv, mo, zif, cw, cb, wqk, bif, mg,
      C0, n0.reshape(Bd, W), m0p)


def _merge_kernel(x_ref, oa_ref, hm_ref, g1_ref, wg_ref, bm_ref, wa_ref, wb_ref, wo_ref, o_ref):
    x = x_ref[...]
    D = x.shape[1]
    ms = jnp.mean(x * x, axis=-1, keepdims=True)
    h = (x * lax.rsqrt(ms + NORM_EPS) * g1_ref[...]).astype(BF16)
    g = jax.nn.sigmoid(_dot(h, wg_ref[...]) + bm_ref[...])
    ya = _dot(oa_ref[...], wa_ref[...])
    yb = _dot(hm_ref[...], wb_ref[...])
    mix = (g[:, 0:D] * ya + g[:, D:2 * D] * yb).astype(BF16)
    o_ref[...] = x + _dot(mix, wo_ref[...])


def _merge(x2, oa, hm, g1, wg, bm, wa, wb, wo, tm):
    M, D = x2.shape
    W = oa.shape[1]
    row = lambda i: (i, 0)
    fixed = lambda i: (0, 0)
    return pl.pallas_call(
        _merge_kernel,
        grid=(M // tm,),
        in_specs=[
            pl.BlockSpec((tm, D), row),
            pl.BlockSpec((tm, W), row),
            pl.BlockSpec((tm, W), row),
            pl.BlockSpec((1, D), fixed),
            pl.BlockSpec(wg.shape, fixed),
            pl.BlockSpec((1, 2 * D), fixed),
            pl.BlockSpec(wa.shape, fixed),
            pl.BlockSpec(wb.shape, fixed),
            pl.BlockSpec(wo.shape, fixed),
        ],
        out_specs=pl.BlockSpec((tm, D), row),
        out_shape=jax.ShapeDtypeStruct((M, D), F32),
        compiler_params=_cparams(("parallel",)),
        name="merge",
    )(x2, oa, hm, g1, wg, bm, wa, wb, wo)


def _ffn_kernel(x_ref, g2_ref, wgate_ref, wup_ref, wdown_ref, o_ref):
    x = x_ref[...]
    ms = jnp.mean(x * x, axis=-1, keepdims=True)
    h = (x * lax.rsqrt(ms + NORM_EPS) * g2_ref[...]).astype(BF16)
    a = _dot(h, wgate_ref[...])
    b = _dot(h, wup_ref[...])
    act = (a * jax.nn.sigmoid(a) * b).astype(BF16)
    o_ref[...] = x + _dot(act, wdown_ref[...])


def _ffn(x2, g2, wgate, wup, wdown, tm):
    M, D = x2.shape
    row = lambda i: (i, 0)
    fixed = lambda i: (0, 0)
    return pl.pallas_call(
        _ffn_kernel,
        grid=(M // tm,),
        in_specs=[
            pl.BlockSpec((tm, D), row),
            pl.BlockSpec((1, D), fixed),
            pl.BlockSpec(wgate.shape, fixed),
            pl.BlockSpec(wup.shape, fixed),
            pl.BlockSpec(wdown.shape, fixed),
        ],
        out_specs=pl.BlockSpec((tm, D), row),
        out_shape=jax.ShapeDtypeStruct((M, D), F32),
        compiler_params=_cparams(("parallel",)),
        name="ffn",
    )(x2, g2, wgate, wup, wdown)


def _rope_tables(pos):
    half = DA_HEAD_DIM // 2
    inv = ROPE_THETA ** (-jnp.arange(half, dtype=F32) / half)
    ang = pos.astype(F32)[:, None] * inv[None, :]
    cos = jnp.cos(ang)
    sin = jnp.sin(ang)
    cos = jnp.concatenate([cos, cos], axis=1)
    sin = jnp.concatenate([-sin, sin], axis=1)
    reps = LANES // DA_HEAD_DIM
    return jnp.concatenate([cos] * reps, axis=1), jnp.concatenate([sin] * reps, axis=1)


def _pick_tile(M, pref):
    t = min(pref, M)
    while M % t:
        t //= 2
    return t


def _layer_weights(w_in, qnorm_g, knorm_g, lambda_q1, lambda_k1, lambda_q2, lambda_k2, subln_g, w_a_out,
                   conv_w, conv_b, w_mq, w_mk, b_igate, b_fgate, mnorm_g, w_b_out, b_merge, w_o,
                   w_ffn_gate, w_ffn_up, w_ffn_down):
    W = DA_HEADS * DA_V_DIM
    n_main = 6 * W
    n_if = 2 * ML_HEADS
    w_if = jnp.pad(w_in[:, n_main:n_main + n_if], ((0, 0), (0, LANES - n_if)))
    p = {}
    p['w_main'] = jnp.concatenate([w_in[:, :n_main], w_if], axis=1).astype(BF16)
    p['w_gates'] = w_in[:, n_main + n_if:].astype(BF16)
    p['qg'] = jnp.tile(qnorm_g, 2 * DA_HEADS)[None, :]
    p['kg'] = jnp.tile(knorm_g, 2 * DA_HEADS)[None, :]
    grp = np.arange(W) // DA_HEAD_DIM
    p['gm'] = jnp.asarray((grp[:, None] == grp[None, :]).astype(np.float32) / DA_HEAD_DIM, dtype=BF16)
    p['lams'] = jnp.stack([lambda_q1, lambda_k1, lambda_q2, lambda_k2]).astype(F32)
    p['sg'] = subln_g[None, :]
    p['w_a_out'] = w_a_out.astype(BF16)
    p['conv_w'] = conv_w
    p['conv_b'] = conv_b[None, :]
    p['wqk'] = jnp.concatenate([w_mq, w_mk], axis=2).astype(BF16)
    p['bif'] = jnp.pad(jnp.concatenate([b_igate, b_fgate]), (0, LANES - n_if))[None, :]
    p['mg'] = mnorm_g[None, :]
    p['w_b_out'] = w_b_out.astype(BF16)
    p['b_merge'] = b_merge[None, :]
    p['w_o'] = w_o.astype(BF16)
    p['w_ffn_gate'] = w_ffn_gate.astype(BF16)
    p['w_ffn_up'] = w_ffn_up.astype(BF16)
    p['w_ffn_down'] = w_ffn_down.astype(BF16)
    return p


def kernel(x_prompt, x_sample, cache_k, cache_v, page_table, state_C, state_n, state_m, state_conv, norm1_g, w_in, qnorm_g, knorm_g, lambda_q1, lambda_k1, lambda_q2, lambda_k2, subln_g, w_a_out, conv_w, conv_b, w_mq, w_mk, b_igate, b_fgate, mnorm_g, w_b_out, b_merge, w_o, norm2_g, w_ffn_gate, w_ffn_up, w_ffn_down):
    Bp, Tp, D = x_prompt.shape
    Bs, Ts, _ = x_sample.shape
    assert Ts == 1, "the sample group decodes one token per sequence"
    depth = w_in.shape[0]
    n_pages = page_table.shape[1]
    page_rows = cache_k.shape[2]
    past_len = n_pages * page_rows
    W = DA_HEADS * DA_V_DIM
    H, HD = ML_HEADS, ML_HEAD_DIM

    Mp = Bp * Tp
    tm_p = _pick_tile(Tp, 512)
    tm_s = _pick_tile(Bs, 128)
    cos_p, sin_p = _rope_tables(jnp.arange(Tp))
    cos_s, sin_s = _rope_tables(past_len + jnp.arange(Ts))
    cos_s = jnp.tile(cos_s, (tm_s, 1))
    sin_s = jnp.tile(sin_s, (tm_s, 1))
    tq = _pick_tile(Tp, 1024)
    rq_full = min(tq, 128)
    rq_diag = min(tq, 256)
    L = _pick_tile(Tp, 256)
    G = _pick_tile(n_pages, 32)
    TB = _pick_tile(Bs, 8)

    yp = x_prompt.reshape(Mp, D)
    ys = x_sample.reshape(Bs, D)
    outs = {k: [] for k in ('kp', 'vp', 'Cp', 'np', 'mp', 'cp', 'ks', 'vs', 'Cs', 'ns', 'ms', 'cs')}
    for l in range(depth):
        p = _layer_weights(w_in[l], qnorm_g[l], knorm_g[l], lambda_q1[l], lambda_k1[l], lambda_q2[l],
                           lambda_k2[l], subln_g[l], w_a_out[l], conv_w[l], conv_b[l], w_mq[l], w_mk[l],
                           b_igate[l], b_fgate[l], mnorm_g[l], w_b_out[l], b_merge[l], w_o[l],
                           w_ffn_gate[l], w_ffn_up[l], w_ffn_down[l])
        g1 = norm1_g[l][None, :]
        g2 = norm2_g[l][None, :]
        lam_init = 0.8 - 0.6 * math.exp(-0.3 * l)

        q, k, kb, v, vb, u, mv, mo, zif = _in_proj(yp, g1, p['w_main'], p['qg'], p['kg'], cos_p, sin_p,
                                                   p['gm'], tm_p, BF16, True)
        oa = _attn_prompt(q.reshape(Bp, Tp, W), kb.reshape(Bp, Tp, W), vb.reshape(Bp, Tp, W),
                          p['lams'], p['sg'], lam_init, tq, rq_full, rq_diag)
        hm, Cp, np_, mp = _mlstm_prompt(u.reshape(Bp, Tp, W), mv.reshape(Bp, Tp, W), mo.reshape(Bp, Tp, W),
                                        zif.reshape(Bp, Tp, LANES), p['conv_w'], p['conv_b'], p['wqk'],
                                        p['bif'], p['mg'], L)
        x1 = _merge(yp, oa.reshape(Mp, W), hm.reshape(Mp, W), g1, p['w_gates'], p['b_merge'],
                    p['w_a_out'], p['w_b_out'], p['w_o'], tm_p)
        yp = _ffn(x1, g2, p['w_ffn_gate'], p['w_ffn_up'], p['w_ffn_down'], tm_p)
        outs['kp'].append(jnp.transpose(k.reshape(Bp, 2 * DA_HEADS, DA_HEAD_DIM, Tp), (0, 3, 1, 2)))
        outs['vp'].append(v.reshape(Bp, Tp, DA_HEADS, DA_V_DIM))
        outs['Cp'].append(Cp)
        outs['np'].append(np_[:, :H, :])
        outs['mp'].append(mp[:, :H, 0])
        outs['cp'].append(u.reshape(Bp, Tp, W)[:, Tp - (CONV_W - 1):, :])

        q, k, kb, v, vb, u, mv, mo, zif = _in_proj(ys, g1, p['w_main'], p['qg'], p['kg'], cos_s, sin_s,
                                                   p['gm'], tm_s, F32, False)
        n_phys = cache_k.shape[1]
        ckt = jnp.transpose(cache_k[l], (0, 2, 3, 1)).reshape(n_phys, W, page_rows)
        cv2 = cache_v[l].reshape(n_phys, page_rows * DA_HEADS, DA_V_DIM)
        oa = _attn_sample(q, k, v, ckt, cv2, page_table, p['lams'], p['sg'], lam_init, G)
        hm, Cs, ns, ms = _mlstm_sample(u, state_conv[l], mv, mo, zif, p['conv_w'], p['conv_b'], p['wqk'],
                                       p['bif'], p['mg'], state_C[l], state_n[l], state_m[l], TB)
        x1 = _merge(ys, oa, hm, g1, p['w_gates'], p['b_merge'], p['w_a_out'], p['w_b_out'], p['w_o'], tm_s)
        ys = _ffn(x1, g2, p['w_ffn_gate'], p['w_ffn_up'], p['w_ffn_down'], tm_s)
        outs['ks'].append(k.reshape(Bs, Ts, 2 * DA_HEADS, DA_HEAD_DIM))
        outs['vs'].append(v.reshape(Bs, Ts, DA_HEADS, DA_V_DIM))
        outs['Cs'].append(Cs)
        outs['ns'].append(ns.reshape(Bs, H, HD))
        outs['ms'].append(ms[:, :H])
        outs['cs'].append(jnp.concatenate([state_conv[l][:, 1:, :], u[:, None, :]], axis=1))

    st = lambda name: jnp.stack(outs[name])
    return (yp.reshape(Bp, Tp, D), ys.reshape(Bs, Ts, D),
            st('kp'), st('vp'), st('Cp'), st('np'), st('mp'), st('cp'),
            st('ks'), st('vs'), st('Cs'), st('ns'), st('ms'), st('cs'))
```

```python
import functools
import math

import jax
import jax.numpy as jnp
import numpy as np
from jax import lax
from jax.experimental import pallas as pl
from jax.experimental.pallas import tpu as pltpu

F32 = jnp.float32
BF16 = jnp.bfloat16

DA_HEADS = 4
DA_HEAD_DIM = 64
DA_V_DIM = 2 * DA_HEAD_DIM
ML_HEADS = 4
ML_HEAD_DIM = 128
CONV_W = 4
ROPE_THETA = 10000.0
NORM_EPS = 1e-6

LANES = 128
SUBLANES = 8
VMEM_LIMIT = 56 * 1024 * 1024

_HIGHEST = lax.Precision.HIGHEST


def _cparams(sem):
    return pltpu.CompilerParams(dimension_semantics=sem, vmem_limit_bytes=VMEM_LIMIT)


def _dot(a, b):
    return jnp.dot(a, b, preferred_element_type=F32)


def _dot_nt(a, b):
    return lax.dot_general(a, b, (((1,), (1,)), ((), ())), preferred_element_type=F32)


def _dot_tn(a, b):
    return lax.dot_general(a, b, (((0,), (0,)), ((), ())), preferred_element_type=F32)


def _log_sigmoid(x):
    return jnp.minimum(x, 0.0) - jnp.log(1.0 + jnp.exp(-jnp.abs(x)))


def _causal_conv_silu(u, ext, cw_ref, cb_ref):
    n = u.shape[0]
    ext[SUBLANES:SUBLANES + n, :] = u
    conv = cb_ref[...]
    for j in range(CONV_W):
        off = SUBLANES - (CONV_W - 1) + j
        conv = conv + ext[off:off + n, :] * cw_ref[j:j + 1, :]
    ext[0:SUBLANES, :] = u[n - SUBLANES:n, :]
    return conv * jax.nn.sigmoid(conv)


def _in_proj_kernel(x_ref, g1_ref, w_ref, qg_ref, kg_ref, cos_ref, sin_ref, gm_ref,
                    q_ref, k_ref, kb_ref, v_ref, vb_ref, u_ref, mv_ref, mo_ref, zif_ref, *, cache_layout):
    x = x_ref[...]
    ms = jnp.mean(x * x, axis=-1, keepdims=True)
    h = (x * lax.rsqrt(ms + NORM_EPS) * g1_ref[...]).astype(BF16)
    z = _dot(h, w_ref[...])
    width = DA_HEADS * DA_V_DIM
    cos = jnp.concatenate([cos_ref[...]] * (width // LANES), axis=1)
    sin = jnp.concatenate([sin_ref[...]] * (width // LANES), axis=1)
    lane = lax.broadcasted_iota(jnp.int32, (x.shape[0], width), 1)
    first_half = (lane % DA_HEAD_DIM) < (DA_HEAD_DIM // 2)

    def qk_norm_rope(t, g):
        ms_g = _dot((t * t).astype(BF16), gm_ref[...])
        y = t * lax.rsqrt(ms_g + NORM_EPS) * g
        partner = jnp.where(first_half,
                            pltpu.roll(y, width - DA_HEAD_DIM // 2, 1),
                            pltpu.roll(y, DA_HEAD_DIM // 2, 1))
        return y * cos + partner * sin

    q = qk_norm_rope(z[:, 0:width], qg_ref[...])
    k = qk_norm_rope(z[:, width:2 * width], kg_ref[...])
    q_ref[...] = (q * (DA_HEAD_DIM ** -0.5)).astype(q_ref.dtype)
    v = z[:, 2 * width:3 * width]
    if cache_layout:
        tm = x.shape[0]
        for j in range(width // LANES):
            k_ref[j * LANES:(j + 1) * LANES, :] = jnp.transpose(k[:, j * LANES:(j + 1) * LANES])
        for hd in range(DA_HEADS):
            v_ref[pl.ds(hd, tm, stride=DA_HEADS), :] = v[:, hd * DA_V_DIM:(hd + 1) * DA_V_DIM]
    else:
        k_ref[...] = k
        v_ref[...] = v
    kb_ref[...] = k.astype(BF16)
    vb_ref[...] = v.astype(BF16)
    u_ref[...] = z[:, 3 * width:4 * width]
    mv_ref[...] = z[:, 4 * width:5 * width].astype(BF16)
    mo_ref[...] = z[:, 5 * width:6 * width]
    zif_ref[...] = z[:, 6 * width:6 * width + LANES]


def _in_proj(x2, g1, w, qg, kg, cos, sin, gm, tm, q_dtype, cache_layout):
    M, D = x2.shape
    W = 512
    tab_blocks = cos.shape[0] // tm
    row = lambda i: (i, 0)
    tab = lambda i: (i % tab_blocks, 0)
    fixed = lambda i: (0, 0)
    if cache_layout:
        R = cos.shape[0]
        k_out = jax.ShapeDtypeStruct((M // R, W, R), F32)
        k_spec = pl.BlockSpec((None, W, tm), lambda i: (i // tab_blocks, 0, i % tab_blocks))
        v_out = jax.ShapeDtypeStruct((M * DA_HEADS, DA_V_DIM), F32)
        v_spec = pl.BlockSpec((tm * DA_HEADS, DA_V_DIM), row)
    else:
        k_out = v_out = jax.ShapeDtypeStruct((M, W), F32)
        k_spec = v_spec = pl.BlockSpec((tm, W), row)
    outs = [
        jax.ShapeDtypeStruct((M, W), q_dtype),
        k_out,
        jax.ShapeDtypeStruct((M, W), BF16),
        v_out,
        jax.ShapeDtypeStruct((M, W), BF16),
        jax.ShapeDtypeStruct((M, W), F32),
        jax.ShapeDtypeStruct((M, W), BF16),
        jax.ShapeDtypeStruct((M, W), F32),
        jax.ShapeDtypeStruct((M, LANES), F32),
    ]
    out_specs = [pl.BlockSpec((tm, o.shape[1]), row) for o in outs]
    out_specs[1] = k_spec
    out_specs[3] = v_spec
    return pl.pallas_call(
        functools.partial(_in_proj_kernel, cache_layout=cache_layout),
        grid=(M // tm,),
        in_specs=[
            pl.BlockSpec((tm, D), row),
            pl.BlockSpec((1, D), fixed),
            pl.BlockSpec(w.shape, fixed),
            pl.BlockSpec((1, W), fixed),
            pl.BlockSpec((1, W), fixed),
            pl.BlockSpec((tm, LANES), tab),
            pl.BlockSpec((tm, LANES), tab),
            pl.BlockSpec((W, W), fixed),
        ],
        out_specs=out_specs,
        out_shape=outs,
        compiler_params=_cparams(("parallel",)),
        name="in_proj",
    )(x2, g1, w, qg, kg, cos, sin, gm)


def _lambda_value(lams, lam_init):
    a = jnp.sum(lams[0:1, :] * lams[1:2, :], axis=1, keepdims=True)
    b = jnp.sum(lams[2:3, :] * lams[3:4, :], axis=1, keepdims=True)
    return jnp.exp(a) - jnp.exp(b) + lam_init


def _lane_tile(x, n):
    return x if n == 1 else jnp.concatenate([x] * n, axis=1)


def _attn_prompt_kernel(qi_tab, ki_tab, lams_ref, q_ref, k_ref, v_ref, sg_ref, o_ref,
                        qs, vx, m_s, acc_s, *, tq, rq_full, rq_diag, lam_init):
    step = pl.program_id(2)
    qi = qi_tab[step]
    ki = ki_tab[step]
    tk = tq

    @pl.when(ki == 0)
    def _():
        q = q_ref[...].astype(F32)
        lane = lax.broadcasted_iota(jnp.int32, q.shape, 1)
        qs[0:tq, :] = jnp.where(lane < DA_HEAD_DIM, q, 0.0).astype(BF16)
        qs[tq:2 * tq, :] = jnp.where(lane >= DA_HEAD_DIM, q, 0.0).astype(BF16)
        vx[:, LANES:2 * LANES] = jnp.ones((tk, LANES), BF16)
        m_s[...] = jnp.full(m_s.shape, -jnp.inf, F32)
        acc_s[...] = jnp.zeros(acc_s.shape, F32)

    def update(masked):
        rq = rq_diag if masked else rq_full
        vx[:, 0:LANES] = v_ref[...]
        for r0 in range(0, 2 * tq, rq):
            off = r0 % tq
            nk = min(tk, off + rq) if masked else tk
            rows = slice(r0, r0 + rq)
            s = _dot_nt(qs[rows, :], k_ref[0:nk, :])
            if masked:
                row = lax.broadcasted_iota(jnp.int32, (rq, rq), 0)
                col = lax.broadcasted_iota(jnp.int32, (rq, rq), 1)
                diag = jnp.where(col <= row, s[:, off:off + rq], -jnp.inf)
                s = diag if off == 0 else jnp.concatenate([s[:, 0:off], diag], axis=1)
            m_prev = m_s[rows, :]
            m_new = jnp.maximum(m_prev, jnp.max(s, axis=1, keepdims=True))
            alpha = jnp.exp(m_prev - m_new)
            p = jnp.exp(s - _lane_tile(m_new, nk // LANES))
            pv = _dot(p.astype(BF16), vx[0:nk, :])
            acc_s[rows, :] = _lane_tile(alpha, 2) * acc_s[rows, :] + pv
            m_s[rows, :] = m_new

    @pl.when(ki < qi)
    def _():
        update(False)

    @pl.when(ki == qi)
    def _():
        update(True)
        o = acc_s[:, 0:LANES] / acc_s[:, LANES:2 * LANES]
        lam = _lambda_value(lams_ref[...], lam_init)
        d = o[0:tq, :] - lam * o[tq:2 * tq, :]
        ms = jnp.mean(d * d, axis=-1, keepdims=True)
        y = d * lax.rsqrt(ms + NORM_EPS) * sg_ref[...] * (1.0 - lam_init)
        o_ref[...] = y.astype(BF16)


def _attn_prompt(q, k, v, lams, sg, lam_init, tq, rq_full, rq_diag):
    B, T, W = q.shape
    nq = T // tq
    qi_tab = np.concatenate([np.full(i + 1, i, np.int32) for i in range(nq)])
    ki_tab = np.concatenate([np.arange(i + 1, dtype=np.int32) for i in range(nq)])
    nsteps = int(qi_tab.shape[0])
    grid_spec = pltpu.PrefetchScalarGridSpec(
        num_scalar_prefetch=2,
        grid=(B, DA_HEADS, nsteps),
        in_specs=[
            pl.BlockSpec((4, DA_HEAD_DIM), lambda b, h, s, qt, kt: (0, 0)),
            pl.BlockSpec((None, tq, LANES), lambda b, h, s, qt, kt: (b, qt[s], h)),
            pl.BlockSpec((None, tq, LANES), lambda b, h, s, qt, kt: (b, kt[s], h)),
            pl.BlockSpec((None, tq, LANES), lambda b, h, s, qt, kt: (b, kt[s], h)),
            pl.BlockSpec((1, LANES), lambda b, h, s, qt, kt: (0, 0)),
        ],
        out_specs=pl.BlockSpec((None, tq, LANES), lambda b, h, s, qt, kt: (b, qt[s], h)),
        scratch_shapes=[
            pltpu.VMEM((2 * tq, LANES), BF16),
            pltpu.VMEM((tq, 2 * LANES), BF16),
            pltpu.VMEM((2 * tq, LANES), F32),
            pltpu.VMEM((2 * tq, 2 * LANES), F32),
        ],
    )
    return pl.pallas_call(
        functools.partial(_attn_prompt_kernel, tq=tq, rq_full=rq_full, rq_diag=rq_diag, lam_init=lam_init),
        grid_spec=grid_spec,
        out_shape=jax.ShapeDtypeStruct((B, T, W), BF16),
        compiler_params=_cparams(("parallel", "parallel", "arbitrary")),
        name="attn_prompt",
    )(jnp.asarray(qi_tab), jnp.asarray(ki_tab), lams, q, k, v, sg)


def _attn_sample_kernel(pt_ref, lams_ref, q_ref, kn_ref, vn_ref, sg_ref, *rest, G, lam_init):
    k_refs = rest[0:G]
    v_refs = rest[G:2 * G]
    o_ref = rest[2 * G]
    qcol, m_s, l_s, acc = rest[2 * G + 1:]
    step = pl.program_id(1)
    W = qcol.shape[0]
    n_maps = 2 * DA_HEADS
    n_lane_blocks = W // LANES

    @pl.when(step == 0)
    def _():
        qb = jnp.broadcast_to(q_ref[...], (LANES, W))
        for j in range(n_lane_blocks):
            qcol[j * LANES:(j + 1) * LANES, :] = jnp.transpose(qb[:, j * LANES:(j + 1) * LANES])
        m_s[...] = jnp.full(m_s.shape, -jnp.inf, F32)
        l_s[...] = jnp.zeros(l_s.shape, F32)
        acc[...] = jnp.zeros(acc.shape, F32)

    qc = qcol[...]
    s_list = []
    for i in range(G):
        prod = k_refs[i][...] * qc
        s_list.append(jnp.sum(prod.reshape(n_maps, DA_HEAD_DIM, LANES), axis=1))
    m_cur = functools.reduce(jnp.maximum, s_list)
    m_prev = m_s[...]
    m_new = jnp.maximum(m_prev, jnp.max(m_cur, axis=1, keepdims=True))
    alpha = jnp.exp(m_prev - m_new)
    p_list = [jnp.exp(s - m_new) for s in s_list]
    l_s[...] = alpha * l_s[...] + functools.reduce(jnp.add, p_list)
    p_all = jnp.concatenate(p_list, axis=1).astype(BF16)
    v_all = jnp.concatenate(
        [jnp.concatenate([v_refs[i][pl.ds(h, LANES, stride=DA_HEADS), :] for h in range(DA_HEADS)], axis=1)
         for i in range(G)], axis=0).astype(BF16)
    acc[...] = _lane_tile(alpha, n_lane_blocks) * acc[...] + _dot(p_all, v_all)
    m_s[...] = m_new

    @pl.when(step == pl.num_programs(1) - 1)
    def _():
        q8 = jnp.broadcast_to(q_ref[...], (n_maps, W))
        r = lax.broadcasted_iota(jnp.int32, (n_maps, W), 0)
        c = lax.broadcasted_iota(jnp.int32, (n_maps, W), 1)
        qm = jnp.where((c // DA_HEAD_DIM) == r, q8, 0.0)
        s_new = jnp.sum(qm * kn_ref[...], axis=1, keepdims=True)
        m_last = m_s[...]
        m_fin = jnp.maximum(m_last, s_new)
        a_fin = jnp.exp(m_last - m_fin)
        p_new = jnp.exp(s_new - m_fin)
        l_tot = jnp.sum(a_fin * l_s[...], axis=1, keepdims=True) + p_new[:, 0:1]
        o8 = (_lane_tile(a_fin, n_lane_blocks) * acc[...]
              + _lane_tile(p_new, n_lane_blocks) * vn_ref[...]) / l_tot
        lam = _lambda_value(lams_ref[...], lam_init)
        parts = []
        for h in range(DA_HEADS):
            blk = o8[:, h * DA_V_DIM:(h + 1) * DA_V_DIM]
            d = blk[2 * h:2 * h + 1, :] - lam * blk[2 * h + 1:2 * h + 2, :]
            ms = jnp.mean(d * d, axis=-1, keepdims=True)
            parts.append(d * lax.rsqrt(ms + NORM_EPS) * sg_ref[...] * (1.0 - lam_init))
        o_ref[...] = jnp.concatenate(parts, axis=1).astype(o_ref.dtype)


def _attn_sample(q, k_new, v_new, cache_kt, cache_v2, page_table, lams, sg, lam_init, G):
    Bd, W = q.shape
    NP = page_table.shape[1]
    assert cache_kt.shape[1:] == (W, LANES) and cache_v2.shape[1:] == (W, LANES)
    pt = page_table.reshape(-1)

    def page_map(i):
        return lambda b, s, pt: (pt[b * NP + s * G + i], 0, 0)

    per_seq = lambda b, s, pt: (b, 0, 0)
    fixed = lambda b, s, pt: (0, 0)
    grid_spec = pltpu.PrefetchScalarGridSpec(
        num_scalar_prefetch=1,
        grid=(Bd, NP // G),
        in_specs=[
            pl.BlockSpec((4, DA_HEAD_DIM), fixed),
            pl.BlockSpec((None, 1, W), per_seq),
            pl.BlockSpec((None, 1, W), per_seq),
            pl.BlockSpec((None, 1, W), per_seq),
            pl.BlockSpec((1, LANES), fixed),
        ] + [pl.BlockSpec((None, W, LANES), page_map(i)) for i in range(G)]
          + [pl.BlockSpec((None, W, LANES), page_map(i)) for i in range(G)],
        out_specs=pl.BlockSpec((None, 1, W), per_seq),
        scratch_shapes=[
            pltpu.VMEM((W, LANES), F32),
            pltpu.VMEM((2 * DA_HEADS, LANES), F32),
            pltpu.VMEM((2 * DA_HEADS, LANES), F32),
            pltpu.VMEM((2 * DA_HEADS, W), F32),
        ],
    )
    out = pl.pallas_call(
        functools.partial(_attn_sample_kernel, G=G, lam_init=lam_init),
        grid_spec=grid_spec,
        out_shape=jax.ShapeDtypeStruct((Bd, 1, W), BF16),
        compiler_params=_cparams(("parallel", "arbitrary")),
        name="attn_sample",
    )(pt, lams, q.reshape(Bd, 1, W), k_new.reshape(Bd, 1, W), v_new.reshape(Bd, 1, W), sg,
      *([cache_kt] * G), *([cache_v2] * G))
    return out.reshape(Bd, W)


def _gate_activations(zif, bif):
    g = zif + bif
    lane = lax.broadcasted_iota(jnp.int32, g.shape, 1)
    return jnp.where(lane < ML_HEADS, g, _log_sigmoid(g))


def _mlstm_prompt_kernel(u_ref, mv_ref, mo_ref, zif_ref, cw_ref, cb_ref, wqk_ref, bif_ref, mg_ref,
                         hm_ref, C_ref, n_ref, m_ref, ext, CN_s, m_s, *, L):
    chunk = pl.program_id(1)
    HD = ML_HEAD_DIM
    nl = L // LANES

    @pl.when(chunk == 0)
    def _():
        ext[0:SUBLANES, :] = jnp.zeros((SUBLANES, ext.shape[1]), F32)
        CN_s[...] = jnp.zeros(CN_s.shape, F32)
        m_s[...] = jnp.zeros(m_s.shape, F32)

    c = _causal_conv_silu(u_ref[...], ext, cw_ref, cb_ref).astype(BF16)
    ones = jnp.ones((L, HD), BF16)

    gact = _gate_activations(zif_ref[...], bif_ref[...])
    gact_t = jnp.transpose(gact)
    row = lax.broadcasted_iota(jnp.int32, (L, L), 0)
    col = lax.broadcasted_iota(jnp.int32, (L, L), 1)
    causal = col <= row
    tri = causal.astype(F32)
    cum_col = jnp.dot(tri, gact, precision=_HIGHEST, preferred_element_type=F32)
    cum_row = lax.dot_general(gact_t[0:SUBLANES, :], tri, (((1,), (1,)), ((), ())),
                              precision=_HIGHEST, preferred_element_type=F32)

    for h in range(ML_HEADS):
        sl = slice(h * HD, (h + 1) * HD)
        b_rep = jnp.broadcast_to(cum_col[:, ML_HEADS + h:ML_HEADS + h + 1], (L, LANES))
        li_rep = jnp.broadcast_to(gact[:, h:h + 1], (L, LANES))
        b_row = cum_row[ML_HEADS + h:ML_HEADS + h + 1, :]
        li_row = gact_t[h:h + 1, :]
        m_prev = m_s[h:h + 1, :]
        dmat = jnp.where(causal, _lane_tile(b_rep, nl) - b_row + li_row, -jnp.inf)
        inter = b_rep + m_prev
        m_t = jnp.maximum(inter, jnp.max(dmat, axis=1, keepdims=True))
        w_intra = jnp.exp(dmat - _lane_tile(m_t, nl))
        w_inter = jnp.exp(inter - m_t)
        qk = _dot(c[:, sl], wqk_ref[h])
        q = qk[:, 0:HD]
        k = qk[:, HD:2 * HD] * (HD ** -0.5)
        qb = q.astype(BF16)
        vx = jnp.concatenate([mv_ref[:, sl], ones], axis=1)
        s = _dot_nt(qb, k.astype(BF16)) * w_intra
        CN = CN_s[h]
        nd = _lane_tile(w_inter, 2) * _dot(qb, CN.astype(BF16)) + _dot(s.astype(BF16), vx)
        hval = nd[:, 0:HD] / jnp.maximum(jnp.abs(nd[:, HD:2 * HD]), jnp.exp(-m_t))
        m_new = m_t[L - 1:L, :]
        b_last = b_rep[L - 1:L, :]
        w_end = jnp.exp(b_last - b_rep + li_rep - m_new)
        decay = jnp.exp(b_last + m_prev - m_new)
        kw = k * w_end
        CN_s[h] = _lane_tile(decay, 2) * CN + _dot_tn(kw.astype(BF16), vx)
        m_s[h:h + 1, :] = m_new
        ms = jnp.mean(hval * hval, axis=-1, keepdims=True)
        y = hval * lax.rsqrt(ms + NORM_EPS) * mg_ref[...]
        hm_ref[:, sl] = (y * jax.nn.sigmoid(mo_ref[:, sl])).astype(BF16)

    @pl.when(chunk == pl.num_programs(1) - 1)
    def _():
        for h in range(ML_HEADS):
            CN = CN_s[h]
            C_ref[h] = CN[:, 0:HD]
            n_ref[h:h + 1, :] = jnp.transpose(CN[:, HD:2 * HD])[0:1, :]
        n_ref[ML_HEADS:SUBLANES, :] = jnp.zeros((SUBLANES - ML_HEADS, HD), F32)
        m_ref[...] = m_s[...]


def _mlstm_prompt(u, mv, mo, zif, cw, cb, wqk, bif, mg, L):
    B, T, W = u.shape
    H, HD = ML_HEADS, ML_HEAD_DIM
    tok = lambda b, j: (b, j, 0)
    fixed2 = lambda b, j: (0, 0)
    outs = [
        jax.ShapeDtypeStruct((B, T, W), BF16),
        jax.ShapeDtypeStruct((B, H, HD, HD), F32),
        jax.ShapeDtypeStruct((B, SUBLANES, HD), F32),
        jax.ShapeDtypeStruct((B, SUBLANES, LANES), F32),
    ]
    return pl.pallas_call(
        functools.partial(_mlstm_prompt_kernel, L=L),
        grid=(B, T // L),
        in_specs=[
            pl.BlockSpec((None, L, W), tok),
            pl.BlockSpec((None, L, W), tok),
            pl.BlockSpec((None, L, W), tok),
            pl.BlockSpec((None, L, LANES), tok),
            pl.BlockSpec((CONV_W, W), fixed2),
            pl.BlockSpec((1, W), fixed2),
            pl.BlockSpec((H, HD, 2 * HD), lambda b, j: (0, 0, 0)),
            pl.BlockSpec((1, LANES), fixed2),
            pl.BlockSpec((1, HD), fixed2),
        ],
        out_specs=[
            pl.BlockSpec((None, L, W), tok),
            pl.BlockSpec((None, H, HD, HD), lambda b, j: (b, 0, 0, 0)),
            pl.BlockSpec((None, SUBLANES, HD), lambda b, j: (b, 0, 0)),
            pl.BlockSpec((None, SUBLANES, LANES), lambda b, j: (b, 0, 0)),
        ],
        out_shape=outs,
        scratch_shapes=[
            pltpu.VMEM((L + SUBLANES, W), F32),
            pltpu.VMEM((H, HD, 2 * HD), F32),
            pltpu.VMEM((SUBLANES, LANES), F32),
        ],
        compiler_params=_cparams(("parallel", "arbitrary")),
        name="mlstm_prompt",
    )(u, mv, mo, zif, cw, cb, wqk, bif, mg)


def _mlstm_sample_kernel(u_ref, c0_ref, c1_ref, c2_ref, mv_ref, mo_ref, zif_ref, cw_ref, cb_ref, wqk_ref,
                         bif_ref, mg_ref, C_in, n_in, m_in, hm_ref, C_out, n_out, m_out, *, TB):
    HD = ML_HEAD_DIM
    conv = (cb_ref[...] + c0_ref[...] * cw_ref[0:1, :] + c1_ref[...] * cw_ref[1:2, :]
            + c2_ref[...] * cw_ref[2:3, :] + u_ref[...] * cw_ref[3:4, :])
    c = (conv * jax.nn.sigmoid(conv)).astype(BF16)
    gact = _gate_activations(zif_ref[...], bif_ref[...])
    m0_all = m_in[...]
    rows = lax.broadcasted_iota(jnp.int32, (TB, HD), 0)
    m_new_all = jnp.zeros((TB, LANES), F32)
    lane = lax.broadcasted_iota(jnp.int32, (TB, LANES), 1)
    for h in range(ML_HEADS):
        sl = slice(h * HD, (h + 1) * HD)
        li = gact[:, h:h + 1]
        lf = gact[:, ML_HEADS + h:ML_HEADS + h + 1]
        m0 = m0_all[:, h:h + 1]
        inter = lf + m0
        m_t = jnp.maximum(inter, li)
        w_intra = jnp.exp(li - m_t)
        w_inter = jnp.exp(inter - m_t)
        qk = _dot(c[:, sl], wqk_ref[h])
        q = qk[:, 0:HD]
        k = qk[:, HD:2 * HD] * (HD ** -0.5)
        v = mv_ref[:, sl].astype(F32)
        n0 = n_in[:, sl]
        s = jnp.sum(q * k, axis=1, keepdims=True) * w_intra
        kw = k * w_intra
        vb = mv_ref[:, sl]
        qC = jnp.zeros((TB, HD), F32)
        for j in range(TB):
            Cj = C_in[j, h]
            only_j = rows == j
            qC = qC + _dot(jnp.where(only_j, q, 0.0).astype(BF16), Cj.astype(BF16))
            outer = _dot_tn(jnp.where(only_j, kw, 0.0).astype(BF16), vb)
            C_out[j, h] = w_inter[j:j + 1, :] * Cj + outer
        num = w_inter * qC + s * v
        den = w_inter * jnp.sum(q * n0, axis=1, keepdims=True) + s
        hval = num / jnp.maximum(jnp.abs(den), jnp.exp(-m_t))
        n_out[:, sl] = w_inter * n0 + kw
        m_new_all = jnp.where(lane == h, m_t, m_new_all)
        ms = jnp.mean(hval * hval, axis=-1, keepdims=True)
        y = hval * lax.rsqrt(ms + NORM_EPS) * mg_ref[...]
        hm_ref[:, sl] = (y * jax.nn.sigmoid(mo_ref[:, sl])).astype(BF16)
    m_out[...] = m_new_all


def _mlstm_sample(u, conv_state, mv, mo, zif, cw, cb, wqk, bif, mg, C0, n0, m0, TB):
    Bd, W = u.shape
    H, HD = ML_HEADS, ML_HEAD_DIM
    row = lambda i: (i, 0)
    fixed = lambda i: (0, 0)
    m0p = jnp.pad(m0, ((0, 0), (0, LANES - H)))
    outs = [
        jax.ShapeDtypeStruct((Bd, W), BF16),
        jax.ShapeDtypeStruct((Bd, H, HD, HD), F32),
        jax.ShapeDtypeStruct((Bd, W), F32),
        jax.ShapeDtypeStruct((Bd, LANES), F32),
    ]
    return pl.pallas_call(
        functools.partial(_mlstm_sample_kernel, TB=TB),
        grid=(Bd // TB,),
        in_specs=[
            pl.BlockSpec((TB, W), row),
            pl.BlockSpec((TB, W), row),
            pl.BlockSpec((TB, W), row),
            pl.BlockSpec((TB, W), row),
            pl.BlockSpec((TB, W), row),
            pl.BlockSpec((TB, W), row),
            pl.BlockSpec((TB, LANES), row),
            pl.BlockSpec((CONV_W, W), fixed),
            pl.BlockSpec((1, W), fixed),
            pl.BlockSpec((H, HD, 2 * HD), lambda i: (0, 0, 0)),
            pl.BlockSpec((1, LANES), fixed),
            pl.BlockSpec((1, HD), fixed),
            pl.BlockSpec((TB, H, HD, HD), lambda i: (i, 0, 0, 0)),
            pl.BlockSpec((TB, W), row),
            pl.BlockSpec((TB, LANES), row),
        ],
        out_specs=[
            pl.BlockSpec((TB, W), row),
            pl.BlockSpec((TB, H, HD, HD), lambda i: (i, 0, 0, 0)),
            pl.BlockSpec((TB, W), row),
            pl.BlockSpec((TB, LANES), row),
        ],
        out_shape=outs,
        compiler_params=_cparams(("parallel",)),
        name="mlstm_sample",
    )(u, conv_state[:, 0], conv_state[:, 1], conv_state[:, 2], mv, mo, zif, cw, cb, wqk, bif, mg,
      C0, n0.reshape(Bd, W), m0p)


def _merge_kernel(x_ref, oa_ref, hm_ref, g1_ref, wg_ref, bm_ref, wa_ref, wb_ref, wo_ref, o_ref):
    x = x_ref[...]
    D = x.shape[1]
    ms = jnp.mean(x * x, axis=-1, keepdims=True)
    h = (x * lax.rsqrt(ms + NORM_EPS) * g1_ref[...]).astype(BF16)
    g = jax.nn.sigmoid(_dot(h, wg_ref[...]) + bm_ref[...])
    ya = _dot(oa_ref[...], wa_ref[...])
    yb = _dot(hm_ref[...], wb_ref[...])
    mix = (g[:, 0:D] * ya + g[:, D:2 * D] * yb).astype(BF16)
    o_ref[...] = x + _dot(mix, wo_ref[...])


def _merge(x2, oa, hm, g1, wg, bm, wa, wb, wo, tm):
    M, D = x2.shape
    W = oa.shape[1]
    row = lambda i: (i, 0)
    fixed = lambda i: (0, 0)
    return pl.pallas_call(
        _merge_kernel,
        grid=(M // tm,),
        in_specs=[
            pl.BlockSpec((tm, D), row),
            pl.BlockSpec((tm, W), row),
            pl.BlockSpec((tm, W), row),
            pl.BlockSpec((1, D), fixed),
            pl.BlockSpec(wg.shape, fixed),
            pl.BlockSpec((1, 2 * D), fixed),
            pl.BlockSpec(wa.shape, fixed),
            pl.BlockSpec(wb.shape, fixed),
            pl.BlockSpec(wo.shape, fixed),
        ],
        out_specs=pl.BlockSpec((tm, D), row),
        out_shape=jax.ShapeDtypeStruct((M, D), F32),
        compiler_params=_cparams(("parallel",)),
        name="merge",
    )(x2, oa, hm, g1, wg, bm, wa, wb, wo)


def _ffn_kernel(x_ref, g2_ref, wgate_ref, wup_ref, wdown_ref, o_ref):
    x = x_ref[...]
    ms = jnp.mean(x * x, axis=-1, keepdims=True)
    h = (x * lax.rsqrt(ms + NORM_EPS) * g2_ref[...]).astype(BF16)
    a = _dot(h, wgate_ref[...])
    b = _dot(h, wup_ref[...])
    act = (a * jax.nn.sigmoid(a) * b).astype(BF16)
    o_ref[...] = x + _dot(act, wdown_ref[...])


def _ffn(x2, g2, wgate, wup, wdown, tm):
    M, D = x2.shape
    row = lambda i: (i, 0)
    fixed = lambda i: (0, 0)
    return pl.pallas_call(
        _ffn_kernel,
        grid=(M // tm,),
        in_specs=[
            pl.BlockSpec((tm, D), row),
            pl.BlockSpec((1, D), fixed),
            pl.BlockSpec(wgate.shape, fixed),
            pl.BlockSpec(wup.shape, fixed),
            pl.BlockSpec(wdown.shape, fixed),
        ],
        out_specs=pl.BlockSpec((tm, D), row),
        out_shape=jax.ShapeDtypeStruct((M, D), F32),
        compiler_params=_cparams(("parallel",)),
        name="ffn",
    )(x2, g2, wgate, wup, wdown)


def _rope_tables(pos):
    half = DA_HEAD_DIM // 2
    inv = ROPE_THETA ** (-jnp.arange(half, dtype=F32) / half)
    ang = pos.astype(F32)[:, None] * inv[None, :]
    cos = jnp.cos(ang)
    sin = jnp.sin(ang)
    cos = jnp.concatenate([cos, cos], axis=1)
    sin = jnp.concatenate([-sin, sin], axis=1)
    reps = LANES // DA_HEAD_DIM
    return jnp.concatenate([cos] * reps, axis=1), jnp.concatenate([sin] * reps, axis=1)


def _pick_tile(M, pref):
    t = min(pref, M)
    while M % t:
        t //= 2
    return t


def _layer_weights(w_in, qnorm_g, knorm_g, lambda_q1, lambda_k1, lambda_q2, lambda_k2, subln_g, w_a_out,
                   conv_w, conv_b, w_mq, w_mk, b_igate, b_fgate, mnorm_g, w_b_out, b_merge, w_o,
                   w_ffn_gate, w_ffn_up, w_ffn_down):
    W = DA_HEADS * DA_V_DIM
    n_main = 6 * W
    n_if = 2 * ML_HEADS
    w_if = jnp.pad(w_in[:, n_main:n_main + n_if], ((0, 0), (0, LANES - n_if)))
    p = {}
    p['w_main'] = jnp.concatenate([w_in[:, :n_main], w_if], axis=1).astype(BF16)
    p['w_gates'] = w_in[:, n_main + n_if:].astype(BF16)
    p['qg'] = jnp.tile(qnorm_g, 2 * DA_HEADS)[None, :]
    p['kg'] = jnp.tile(knorm_g, 2 * DA_HEADS)[None, :]
    grp = np.arange(W) // DA_HEAD_DIM
    p['gm'] = jnp.asarray((grp[:, None] == grp[None, :]).astype(np.float32) / DA_HEAD_DIM, dtype=BF16)
    p['lams'] = jnp.stack([lambda_q1, lambda_k1, lambda_q2, lambda_k2]).astype(F32)
    p['sg'] = subln_g[None, :]
    p['w_a_out'] = w_a_out.astype(BF16)
    p['conv_w'] = conv_w
    p['conv_b'] = conv_b[None, :]
    p['wqk'] = jnp.concatenate([w_mq, w_mk], axis=2).astype(BF16)
    p['bif'] = jnp.pad(jnp.concatenate([b_igate, b_fgate]), (0, LANES - n_if))[None, :]
    p['mg'] = mnorm_g[None, :]
    p['w_b_out'] = w_b_out.astype(BF16)
    p['b_merge'] = b_merge[None, :]
    p['w_o'] = w_o.astype(BF16)
    p['w_ffn_gate'] = w_ffn_gate.astype(BF16)
    p['w_ffn_up'] = w_ffn_up.astype(BF16)
    p['w_ffn_down'] = w_ffn_down.astype(BF16)
    return p


def kernel(x_prompt, x_sample, cache_k, cache_v, page_table, state_C, state_n, state_m, state_conv, norm1_g, w_in, qnorm_g, knorm_g, lambda_q1, lambda_k1, lambda_q2, lambda_k2, subln_g, w_a_out, conv_w, conv_b, w_mq, w_mk, b_igate, b_fgate, mnorm_g, w_b_out, b_merge, w_o, norm2_g, w_ffn_gate, w_ffn_up, w_ffn_down):
    Bp, Tp, D = x_prompt.shape
    Bs, Ts, _ = x_sample.shape
    assert Ts == 1, "the sample group decodes one token per sequence"
    depth = w_in.shape[0]
    n_pages = page_table.shape[1]
    page_rows = cache_k.shape[2]
    past_len = n_pages * page_rows
    W = DA_HEADS * DA_V_DIM
    H, HD = ML_HEADS, ML_HEAD_DIM

    Mp = Bp * Tp
    tm_p = _pick_tile(Tp, 512)
    tm_s = _pick_tile(Bs, 128)
    cos_p, sin_p = _rope_tables(jnp.arange(Tp))
    cos_s, sin_s = _rope_tables(past_len + jnp.arange(Ts))
    cos_s = jnp.tile(cos_s, (tm_s, 1))
    sin_s = jnp.tile(sin_s, (tm_s, 1))
    tq = _pick_tile(Tp, 2048)
    rq_full = min(tq, 128)
    rq_diag = min(tq, 512)
    L = _pick_tile(Tp, 256)
    G = _pick_tile(n_pages, 32)
    TB = _pick_tile(Bs, 8)

    yp = x_prompt.reshape(Mp, D)
    ys = x_sample.reshape(Bs, D)
    outs = {k: [] for k in ('kp', 'vp', 'Cp', 'np', 'mp', 'cp', 'ks', 'vs', 'Cs', 'ns', 'ms', 'cs')}
    for l in range(depth):
        p = _layer_weights(w_in[l], qnorm_g[l], knorm_g[l], lambda_q1[l], lambda_k1[l], lambda_q2[l],
                           lambda_k2[l], subln_g[l], w_a_out[l], conv_w[l], conv_b[l], w_mq[l], w_mk[l],
                           b_igate[l], b_fgate[l], mnorm_g[l], w_b_out[l], b_merge[l], w_o[l],
                           w_ffn_gate[l], w_ffn_up[l], w_ffn_down[l])
        g1 = norm1_g[l][None, :]
        g2 = norm2_g[l][None, :]
        lam_init = 0.8 - 0.6 * math.exp(-0.3 * l)

        q, k, kb, v, vb, u, mv, mo, zif = _in_proj(yp, g1, p['w_main'], p['qg'], p['kg'], cos_p, sin_p,
                                                   p['gm'], tm_p, BF16, True)
        oa = _attn_prompt(q.reshape(Bp, Tp, W), kb.reshape(Bp, Tp, W), vb.reshape(Bp, Tp, W),
                          p['lams'], p['sg'], lam_init, tq, rq_full, rq_diag)
        hm, Cp, np_, mp = _mlstm_prompt(u.reshape(Bp, Tp, W), mv.reshape(Bp, Tp, W), mo.reshape(Bp, Tp, W),
                                        zif.reshape(Bp, Tp, LANES), p['conv_w'], p['conv_b'], p['wqk'],
                                        p['bif'], p['mg'], L)
        x1 = _merge(yp, oa.reshape(Mp, W), hm.reshape(Mp, W), g1, p['w_gates'], p['b_merge'],
                    p['w_a_out'], p['w_b_out'], p['w_o'], tm_p)
        yp = _ffn(x1, g2, p['w_ffn_gate'], p['w_ffn_up'], p['w_ffn_down'], tm_p)
        outs['kp'].append(jnp.transpose(k.reshape(Bp, 2 * DA_HEADS, DA_HEAD_DIM, Tp), (0, 3, 1, 2)))
        outs['vp'].append(v.reshape(Bp, Tp, DA_HEADS, DA_V_DIM))
        outs['Cp'].append(Cp)
        outs['np'].append(np_[:, :H, :])
        outs['mp'].append(mp[:, :H, 0])
        outs['cp'].append(u.reshape(Bp, Tp, W)[:, Tp - (CONV_W - 1):, :])

        q, k, kb, v, vb, u, mv, mo, zif = _in_proj(ys, g1, p['w_main'], p['qg'], p['kg'], cos_s, sin_s,
                                                   p['gm'], tm_s, F32, False)
        n_phys = cache_k.shape[1]
        ckt = jnp.transpose(cache_k[l], (0, 2, 3, 1)).reshape(n_phys, W, page_rows)
        cv2 = cache_v[l].reshape(n_phys, page_rows * DA_HEADS, DA_V_DIM)
        oa = _attn_sample(q, k, v, ckt, cv2, page_table, p['lams'], p['sg'], lam_init, G)
        hm, Cs, ns, ms = _mlstm_sample(u, state_conv[l], mv, mo, zif, p['conv_w'], p['conv_b'], p['wqk'],
                                       p['bif'], p['mg'], state_C[l], state_n[l], state_m[l], TB)
        x1 = _merge(ys, oa, hm, g1, p['w_gates'], p['b_merge'], p['w_a_out'], p['w_b_out'], p['w_o'], tm_s)
        ys = _ffn(x1, g2, p['w_ffn_gate'], p['w_ffn_up'], p['w_ffn_down'], tm_s)
        outs['ks'].append(k.reshape(Bs, Ts, 2 * DA_HEADS, DA_HEAD_DIM))
        outs['vs'].append(v.reshape(Bs, Ts, DA_HEADS, DA_V_DIM))
        outs['Cs'].append(Cs)
        outs['ns'].append(ns.reshape(Bs, H, HD))
        outs['ms'].append(ms[:, :H])
        outs['cs'].append(jnp.concatenate([state_conv[l][:, 1:, :], u[:, None, :]], axis=1))

    st = lambda name: jnp.stack(outs[name])
    return (yp.reshape(Bp, Tp, D), ys.reshape(Bs, Ts, D),
            st('kp'), st('vp'), st('Cp'), st('np'), st('mp'), st('cp'),
            st('ks'), st('vs'), st('Cs'), st('ns'), st('ms'), st('cs'))
```

```python
import functools
import math

import jax
import jax.numpy as jnp
import numpy as np
from jax import lax
from jax.experimental import pallas as pl
from jax.experimental.pallas import tpu as pltpu

F32 = jnp.float32
BF16 = jnp.bfloat16

DA_HEADS = 4
DA_HEAD_DIM = 64
DA_V_DIM = 2 * DA_HEAD_DIM
ML_HEADS = 4
ML_HEAD_DIM = 128
CONV_W = 4
ROPE_THETA = 10000.0
NORM_EPS = 1e-6

LANES = 128
SUBLANES = 8
VMEM_LIMIT = 56 * 1024 * 1024

_HIGHEST = lax.Precision.HIGHEST


def _cparams(sem):
    return pltpu.CompilerParams(dimension_semantics=sem, vmem_limit_bytes=VMEM_LIMIT)


def _dot(a, b):
    return jnp.dot(a, b, preferred_element_type=F32)


def _dot_nt(a, b):
    return lax.dot_general(a, b, (((1,), (1,)), ((), ())), preferred_element_type=F32)


def _dot_tn(a, b):
    return lax.dot_general(a, b, (((0,), (0,)), ((), ())), preferred_element_type=F32)


def _log_sigmoid(x):
    return jnp.minimum(x, 0.0) - jnp.log(1.0 + jnp.exp(-jnp.abs(x)))


def _causal_conv_silu(u, ext, cw_ref, cb_ref):
    n = u.shape[0]
    ext[SUBLANES:SUBLANES + n, :] = u
    conv = cb_ref[...]
    for j in range(CONV_W):
        off = SUBLANES - (CONV_W - 1) + j
        conv = conv + ext[off:off + n, :] * cw_ref[j:j + 1, :]
    ext[0:SUBLANES, :] = u[n - SUBLANES:n, :]
    return conv * jax.nn.sigmoid(conv)


def _in_proj_kernel(x_ref, g1_ref, w_ref, qg_ref, kg_ref, cos_ref, sin_ref, gm_ref,
                    q_ref, k_ref, kb_ref, v_ref, vb_ref, u_ref, mv_ref, mo_ref, zif_ref, *, cache_layout):
    x = x_ref[...]
    ms = jnp.mean(x * x, axis=-1, keepdims=True)
    h = (x * lax.rsqrt(ms + NORM_EPS) * g1_ref[...]).astype(BF16)
    width = DA_HEADS * DA_V_DIM
    z_qk = _dot(h, w_ref[:, 0:2 * width])
    cos = jnp.concatenate([cos_ref[...]] * (width // LANES), axis=1)
    sin = jnp.concatenate([sin_ref[...]] * (width // LANES), axis=1)
    lane = lax.broadcasted_iota(jnp.int32, (x.shape[0], width), 1)
    first_half = (lane % DA_HEAD_DIM) < (DA_HEAD_DIM // 2)

    def qk_norm_rope(t, g):
        ms_g = _dot((t * t).astype(BF16), gm_ref[...])
        y = t * lax.rsqrt(ms_g + NORM_EPS) * g
        partner = jnp.where(first_half,
                            pltpu.roll(y, width - DA_HEAD_DIM // 2, 1),
                            pltpu.roll(y, DA_HEAD_DIM // 2, 1))
        return y * cos + partner * sin

    q = qk_norm_rope(z_qk[:, 0:width], qg_ref[...])
    k = qk_norm_rope(z_qk[:, width:2 * width], kg_ref[...])
    z = _dot(h, w_ref[:, 2 * width:])
    q_ref[...] = (q * (DA_HEAD_DIM ** -0.5)).astype(q_ref.dtype)
    v = z[:, 0:width]
    if cache_layout:
        tm = x.shape[0]
        for j in range(width // LANES):
            k_ref[j * LANES:(j + 1) * LANES, :] = jnp.transpose(k[:, j * LANES:(j + 1) * LANES])
        for hd in range(DA_HEADS):
            v_ref[pl.ds(hd, tm, stride=DA_HEADS), :] = v[:, hd * DA_V_DIM:(hd + 1) * DA_V_DIM]
    else:
        k_ref[...] = k
        v_ref[...] = v
    kb_ref[...] = k.astype(BF16)
    vb_ref[...] = v.astype(BF16)
    u_ref[...] = z[:, width:2 * width]
    mv_ref[...] = z[:, 2 * width:3 * width].astype(BF16)
    mo_ref[...] = z[:, 3 * width:4 * width]
    zif_ref[...] = z[:, 4 * width:4 * width + LANES]


def _in_proj(x2, g1, w, qg, kg, cos, sin, gm, tm, q_dtype, cache_layout):
    M, D = x2.shape
    W = 512
    tab_blocks = cos.shape[0] // tm
    row = lambda i: (i, 0)
    tab = lambda i: (i % tab_blocks, 0)
    fixed = lambda i: (0, 0)
    if cache_layout:
        R = cos.shape[0]
        k_out = jax.ShapeDtypeStruct((M // R, W, R), F32)
        k_spec = pl.BlockSpec((None, W, tm), lambda i: (i // tab_blocks, 0, i % tab_blocks))
        v_out = jax.ShapeDtypeStruct((M * DA_HEADS, DA_V_DIM), F32)
        v_spec = pl.BlockSpec((tm * DA_HEADS, DA_V_DIM), row)
    else:
        k_out = v_out = jax.ShapeDtypeStruct((M, W), F32)
        k_spec = v_spec = pl.BlockSpec((tm, W), row)
    outs = [
        jax.ShapeDtypeStruct((M, W), q_dtype),
        k_out,
        jax.ShapeDtypeStruct((M, W), BF16),
        v_out,
        jax.ShapeDtypeStruct((M, W), BF16),
        jax.ShapeDtypeStruct((M, W), F32),
        jax.ShapeDtypeStruct((M, W), BF16),
        jax.ShapeDtypeStruct((M, W), F32),
        jax.ShapeDtypeStruct((M, LANES), F32),
    ]
    out_specs = [pl.BlockSpec((tm, o.shape[1]), row) for o in outs]
    out_specs[1] = k_spec
    out_specs[3] = v_spec
    return pl.pallas_call(
        functools.partial(_in_proj_kernel, cache_layout=cache_layout),
        grid=(M // tm,),
        in_specs=[
            pl.BlockSpec((tm, D), row),
            pl.BlockSpec((1, D), fixed),
            pl.BlockSpec(w.shape, fixed),
            pl.BlockSpec((1, W), fixed),
            pl.BlockSpec((1, W), fixed),
            pl.BlockSpec((tm, LANES), tab),
            pl.BlockSpec((tm, LANES), tab),
            pl.BlockSpec((W, W), fixed),
        ],
        out_specs=out_specs,
        out_shape=outs,
        compiler_params=_cparams(("parallel",)),
        name="in_proj",
    )(x2, g1, w, qg, kg, cos, sin, gm)


def _lambda_value(lams, lam_init):
    a = jnp.sum(lams[0:1, :] * lams[1:2, :], axis=1, keepdims=True)
    b = jnp.sum(lams[2:3, :] * lams[3:4, :], axis=1, keepdims=True)
    return jnp.exp(a) - jnp.exp(b) + lam_init


def _lane_tile(x, n):
    return x if n == 1 else jnp.concatenate([x] * n, axis=1)


def _attn_prompt_kernel(qi_tab, ki_tab, lams_ref, q_ref, k_ref, v_ref, sg_ref, o_ref,
                        qs, vx, m_s, acc_s, *, tq, rq_full, rq_diag, lam_init):
    step = pl.program_id(2)
    qi = qi_tab[step]
    ki = ki_tab[step]
    tk = tq

    @pl.when(ki == 0)
    def _():
        q = q_ref[...].astype(F32)
        lane = lax.broadcasted_iota(jnp.int32, q.shape, 1)
        qs[0:tq, :] = jnp.where(lane < DA_HEAD_DIM, q, 0.0).astype(BF16)
        qs[tq:2 * tq, :] = jnp.where(lane >= DA_HEAD_DIM, q, 0.0).astype(BF16)
        vx[:, LANES:2 * LANES] = jnp.ones((tk, LANES), BF16)
        m_s[...] = jnp.full(m_s.shape, -jnp.inf, F32)
        acc_s[...] = jnp.zeros(acc_s.shape, F32)

    def update(masked):
        rq = rq_diag if masked else rq_full
        vx[:, 0:LANES] = v_ref[...]
        for r0 in range(0, 2 * tq, rq):
            off = r0 % tq
            nk = min(tk, off + rq) if masked else tk
            rows = slice(r0, r0 + rq)
            s = _dot_nt(qs[rows, :], k_ref[0:nk, :])
            if masked:
                row = lax.broadcasted_iota(jnp.int32, (rq, rq), 0)
                col = lax.broadcasted_iota(jnp.int32, (rq, rq), 1)
                diag = jnp.where(col <= row, s[:, off:off + rq], -jnp.inf)
                s = diag if off == 0 else jnp.concatenate([s[:, 0:off], diag], axis=1)
            m_prev = m_s[rows, :]
            m_new = jnp.maximum(m_prev, jnp.max(s, axis=1, keepdims=True))
            alpha = jnp.exp(m_prev - m_new)
            p = jnp.exp(s - _lane_tile(m_new, nk // LANES))
            pv = _dot(p.astype(BF16), vx[0:nk, :])
            acc_s[rows, :] = _lane_tile(alpha, 2) * acc_s[rows, :] + pv
            m_s[rows, :] = m_new

    @pl.when(ki < qi)
    def _():
        update(False)

    @pl.when(ki == qi)
    def _():
        update(True)
        o = acc_s[:, 0:LANES] / acc_s[:, LANES:2 * LANES]
        lam = _lambda_value(lams_ref[...], lam_init)
        d = o[0:tq, :] - lam * o[tq:2 * tq, :]
        ms = jnp.mean(d * d, axis=-1, keepdims=True)
        y = d * lax.rsqrt(ms + NORM_EPS) * sg_ref[...] * (1.0 - lam_init)
        o_ref[...] = y.astype(BF16)


def _attn_prompt(q, k, v, lams, sg, lam_init, tq, rq_full, rq_diag):
    B, T, W = q.shape
    nq = T // tq
    qi_tab = np.concatenate([np.full(i + 1, i, np.int32) for i in range(nq)])
    ki_tab = np.concatenate([np.arange(i + 1, dtype=np.int32) for i in range(nq)])
    nsteps = int(qi_tab.shape[0])
    grid_spec = pltpu.PrefetchScalarGridSpec(
        num_scalar_prefetch=2,
        grid=(B, DA_HEADS, nsteps),
        in_specs=[
            pl.BlockSpec((4, DA_HEAD_DIM), lambda b, h, s, qt, kt: (0, 0)),
            pl.BlockSpec((None, tq, LANES), lambda b, h, s, qt, kt: (b, qt[s], h)),
            pl.BlockSpec((None, tq, LANES), lambda b, h, s, qt, kt: (b, kt[s], h)),
            pl.BlockSpec((None, tq, LANES), lambda b, h, s, qt, kt: (b, kt[s], h)),
            pl.BlockSpec((1, LANES), lambda b, h, s, qt, kt: (0, 0)),
        ],
        out_specs=pl.BlockSpec((None, tq, LANES), lambda b, h, s, qt, kt: (b, qt[s], h)),
        scratch_shapes=[
            pltpu.VMEM((2 * tq, LANES), BF16),
            pltpu.VMEM((tq, 2 * LANES), BF16),
            pltpu.VMEM((2 * tq, LANES), F32),
            pltpu.VMEM((2 * tq, 2 * LANES), F32),
        ],
    )
    return pl.pallas_call(
        functools.partial(_attn_prompt_kernel, tq=tq, rq_full=rq_full, rq_diag=rq_diag, lam_init=lam_init),
        grid_spec=grid_spec,
        out_shape=jax.ShapeDtypeStruct((B, T, W), BF16),
        compiler_params=_cparams(("parallel", "parallel", "arbitrary")),
        name="attn_prompt",
    )(jnp.asarray(qi_tab), jnp.asarray(ki_tab), lams, q, k, v, sg)


def _attn_sample_kernel(pt_ref, lams_ref, q_ref, kn_ref, vn_ref, sg_ref, *rest, G, lam_init):
    k_refs = rest[0:G]
    v_refs = rest[G:2 * G]
    o_ref = rest[2 * G]
    qcol, m_s, l_s, acc = rest[2 * G + 1:]
    step = pl.program_id(1)
    W = qcol.shape[0]
    n_maps = 2 * DA_HEADS
    n_lane_blocks = W // LANES

    @pl.when(step == 0)
    def _():
        qb = jnp.broadcast_to(q_ref[...], (LANES, W))
        for j in range(n_lane_blocks):
            qcol[j * LANES:(j + 1) * LANES, :] = jnp.transpose(qb[:, j * LANES:(j + 1) * LANES])
        m_s[...] = jnp.full(m_s.shape, -jnp.inf, F32)
        l_s[...] = jnp.zeros(l_s.shape, F32)
        acc[...] = jnp.zeros(acc.shape, F32)

    qc = qcol[...]
    s_list = []
    for i in range(G):
        prod = k_refs[i][...] * qc
        s_list.append(jnp.sum(prod.reshape(n_maps, DA_HEAD_DIM, LANES), axis=1))
    m_cur = functools.reduce(jnp.maximum, s_list)
    m_prev = m_s[...]
    m_new = jnp.maximum(m_prev, jnp.max(m_cur, axis=1, keepdims=True))
    alpha = jnp.exp(m_prev - m_new)
    p_list = [jnp.exp(s - m_new) for s in s_list]
    l_s[...] = alpha * l_s[...] + functools.reduce(jnp.add, p_list)
    p_all = jnp.concatenate(p_list, axis=1).astype(BF16)
    v_all = jnp.concatenate(
        [jnp.concatenate([v_refs[i][pl.ds(h, LANES, stride=DA_HEADS), :] for h in range(DA_HEADS)], axis=1)
         for i in range(G)], axis=0).astype(BF16)
    acc[...] = _lane_tile(alpha, n_lane_blocks) * acc[...] + _dot(p_all, v_all)
    m_s[...] = m_new

    @pl.when(step == pl.num_programs(1) - 1)
    def _():
        q8 = jnp.broadcast_to(q_ref[...], (n_maps, W))
        r = lax.broadcasted_iota(jnp.int32, (n_maps, W), 0)
        c = lax.broadcasted_iota(jnp.int32, (n_maps, W), 1)
        qm = jnp.where((c // DA_HEAD_DIM) == r, q8, 0.0)
        s_new = jnp.sum(qm * kn_ref[...], axis=1, keepdims=True)
        m_last = m_s[...]
        m_fin = jnp.maximum(m_last, s_new)
        a_fin = jnp.exp(m_last - m_fin)
        p_new = jnp.exp(s_new - m_fin)
        l_tot = jnp.sum(a_fin * l_s[...], axis=1, keepdims=True) + p_new[:, 0:1]
        o8 = (_lane_tile(a_fin, n_lane_blocks) * acc[...]
              + _lane_tile(p_new, n_lane_blocks) * vn_ref[...]) / l_tot
        lam = _lambda_value(lams_ref[...], lam_init)
        parts = []
        for h in range(DA_HEADS):
            blk = o8[:, h * DA_V_DIM:(h + 1) * DA_V_DIM]
            d = blk[2 * h:2 * h + 1, :] - lam * blk[2 * h + 1:2 * h + 2, :]
            ms = jnp.mean(d * d, axis=-1, keepdims=True)
            parts.append(d * lax.rsqrt(ms + NORM_EPS) * sg_ref[...] * (1.0 - lam_init))
        o_ref[...] = jnp.concatenate(parts, axis=1).astype(o_ref.dtype)


def _attn_sample(q, k_new, v_new, cache_kt, cache_v2, page_table, lams, sg, lam_init, G):
    Bd, W = q.shape
    NP = page_table.shape[1]
    assert cache_kt.shape[1:] == (W, LANES) and cache_v2.shape[1:] == (W, LANES)
    pt = page_table.reshape(-1)

    def page_map(i):
        return lambda b, s, pt: (pt[b * NP + s * G + i], 0, 0)

    per_seq = lambda b, s, pt: (b, 0, 0)
    fixed = lambda b, s, pt: (0, 0)
    grid_spec = pltpu.PrefetchScalarGridSpec(
        num_scalar_prefetch=1,
        grid=(Bd, NP // G),
        in_specs=[
            pl.BlockSpec((4, DA_HEAD_DIM), fixed),
            pl.BlockSpec((None, 1, W), per_seq),
            pl.BlockSpec((None, 1, W), per_seq),
            pl.BlockSpec((None, 1, W), per_seq),
            pl.BlockSpec((1, LANES), fixed),
        ] + [pl.BlockSpec((None, W, LANES), page_map(i)) for i in range(G)]
          + [pl.BlockSpec((None, W, LANES), page_map(i)) for i in range(G)],
        out_specs=pl.BlockSpec((None, 1, W), per_seq),
        scratch_shapes=[
            pltpu.VMEM((W, LANES), F32),
            pltpu.VMEM((2 * DA_HEADS, LANES), F32),
            pltpu.VMEM((2 * DA_HEADS, LANES), F32),
            pltpu.VMEM((2 * DA_HEADS, W), F32),
        ],
    )
    out = pl.pallas_call(
        functools.partial(_attn_sample_kernel, G=G, lam_init=lam_init),
        grid_spec=grid_spec,
        out_shape=jax.ShapeDtypeStruct((Bd, 1, W), BF16),
        compiler_params=_cparams(("parallel", "arbitrary")),
        name="attn_sample",
    )(pt, lams, q.reshape(Bd, 1, W), k_new.reshape(Bd, 1, W), v_new.reshape(Bd, 1, W), sg,
      *([cache_kt] * G), *([cache_v2] * G))
    return out.reshape(Bd, W)


def _gate_activations(zif, bif):
    g = zif + bif
    lane = lax.broadcasted_iota(jnp.int32, g.shape, 1)
    return jnp.where(lane < ML_HEADS, g, _log_sigmoid(g))


def _mlstm_prompt_kernel(u_ref, mv_ref, mo_ref, zif_ref, cw_ref, cb_ref, wqk_ref, bif_ref, mg_ref,
                         hm_ref, C_ref, n_ref, m_ref, ext, CN_s, m_s, *, L):
    chunk = pl.program_id(1)
    HD = ML_HEAD_DIM
    nl = L // LANES

    @pl.when(chunk == 0)
    def _():
        ext[0:SUBLANES, :] = jnp.zeros((SUBLANES, ext.shape[1]), F32)
        CN_s[...] = jnp.zeros(CN_s.shape, F32)
        m_s[...] = jnp.zeros(m_s.shape, F32)

    c = _causal_conv_silu(u_ref[...], ext, cw_ref, cb_ref).astype(BF16)
    ones = jnp.ones((L, HD), BF16)

    gact = _gate_activations(zif_ref[...], bif_ref[...])
    gact_t = jnp.transpose(gact)
    row = lax.broadcasted_iota(jnp.int32, (L, L), 0)
    col = lax.broadcasted_iota(jnp.int32, (L, L), 1)
    causal = col <= row
    tri = causal.astype(F32)
    cum_col = jnp.dot(tri, gact, precision=_HIGHEST, preferred_element_type=F32)
    cum_row = lax.dot_general(gact_t[0:SUBLANES, :], tri, (((1,), (1,)), ((), ())),
                              precision=_HIGHEST, preferred_element_type=F32)

    for h in range(ML_HEADS):
        sl = slice(h * HD, (h + 1) * HD)
        b_rep = jnp.broadcast_to(cum_col[:, ML_HEADS + h:ML_HEADS + h + 1], (L, LANES))
        li_rep = jnp.broadcast_to(gact[:, h:h + 1], (L, LANES))
        b_row = cum_row[ML_HEADS + h:ML_HEADS + h + 1, :]
        li_row = gact_t[h:h + 1, :]
        m_prev = m_s[h:h + 1, :]
        dmat = jnp.where(causal, _lane_tile(b_rep, nl) - b_row + li_row, -jnp.inf)
        inter = b_rep + m_prev
        m_t = jnp.maximum(inter, jnp.max(dmat, axis=1, keepdims=True))
        w_intra = jnp.exp(dmat - _lane_tile(m_t, nl))
        w_inter = jnp.exp(inter - m_t)
        qk = _dot(c[:, sl], wqk_ref[h])
        q = qk[:, 0:HD]
        k = qk[:, HD:2 * HD] * (HD ** -0.5)
        qb = q.astype(BF16)
        vx = jnp.concatenate([mv_ref[:, sl], ones], axis=1)
        s = _dot_nt(qb, k.astype(BF16)) * w_intra
        CN = CN_s[h]
        nd = _lane_tile(w_inter, 2) * _dot(qb, CN.astype(BF16)) + _dot(s.astype(BF16), vx)
        hval = nd[:, 0:HD] / jnp.maximum(jnp.abs(nd[:, HD:2 * HD]), jnp.exp(-m_t))
        m_new = m_t[L - 1:L, :]
        b_last = b_rep[L - 1:L, :]
        w_end = jnp.exp(b_last - b_rep + li_rep - m_new)
        decay = jnp.exp(b_last + m_prev - m_new)
        kw = k * w_end
        CN_s[h] = _lane_tile(decay, 2) * CN + _dot_tn(kw.astype(BF16), vx)
        m_s[h:h + 1, :] = m_new
        ms = jnp.mean(hval * hval, axis=-1, keepdims=True)
        y = hval * lax.rsqrt(ms + NORM_EPS) * mg_ref[...]
        hm_ref[:, sl] = (y * jax.nn.sigmoid(mo_ref[:, sl])).astype(BF16)

    @pl.when(chunk == pl.num_programs(1) - 1)
    def _():
        for h in range(ML_HEADS):
            CN = CN_s[h]
            C_ref[h] = CN[:, 0:HD]
            n_ref[h:h + 1, :] = jnp.transpose(CN[:, HD:2 * HD])[0:1, :]
        n_ref[ML_HEADS:SUBLANES, :] = jnp.zeros((SUBLANES - ML_HEADS, HD), F32)
        m_ref[...] = m_s[...]


def _mlstm_prompt(u, mv, mo, zif, cw, cb, wqk, bif, mg, L):
    B, T, W = u.shape
    H, HD = ML_HEADS, ML_HEAD_DIM
    tok = lambda b, j: (b, j, 0)
    fixed2 = lambda b, j: (0, 0)
    outs = [
        jax.ShapeDtypeStruct((B, T, W), BF16),
        jax.ShapeDtypeStruct((B, H, HD, HD), F32),
        jax.ShapeDtypeStruct((B, SUBLANES, HD), F32),
        jax.ShapeDtypeStruct((B, SUBLANES, LANES), F32),
    ]
    return pl.pallas_call(
        functools.partial(_mlstm_prompt_kernel, L=L),
        grid=(B, T // L),
        in_specs=[
            pl.BlockSpec((None, L, W), tok),
            pl.BlockSpec((None, L, W), tok),
            pl.BlockSpec((None, L, W), tok),
            pl.BlockSpec((None, L, LANES), tok),
            pl.BlockSpec((CONV_W, W), fixed2),
            pl.BlockSpec((1, W), fixed2),
            pl.BlockSpec((H, HD, 2 * HD), lambda b, j: (0, 0, 0)),
            pl.BlockSpec((1, LANES), fixed2),
            pl.BlockSpec((1, HD), fixed2),
        ],
        out_specs=[
            pl.BlockSpec((None, L, W), tok),
            pl.BlockSpec((None, H, HD, HD), lambda b, j: (b, 0, 0, 0)),
            pl.BlockSpec((None, SUBLANES, HD), lambda b, j: (b, 0, 0)),
            pl.BlockSpec((None, SUBLANES, LANES), lambda b, j: (b, 0, 0)),
        ],
        out_shape=outs,
        scratch_shapes=[
            pltpu.VMEM((L + SUBLANES, W), F32),
            pltpu.VMEM((H, HD, 2 * HD), F32),
            pltpu.VMEM((SUBLANES, LANES), F32),
        ],
        compiler_params=_cparams(("parallel", "arbitrary")),
        name="mlstm_prompt",
    )(u, mv, mo, zif, cw, cb, wqk, bif, mg)


def _mlstm_sample_kernel(u_ref, c0_ref, c1_ref, c2_ref, mv_ref, mo_ref, zif_ref, cw_ref, cb_ref, wqk_ref,
                         bif_ref, mg_ref, C_in, n_in, m_in, hm_ref, C_out, n_out, m_out, *, TB):
    HD = ML_HEAD_DIM
    conv = (cb_ref[...] + c0_ref[...] * cw_ref[0:1, :] + c1_ref[...] * cw_ref[1:2, :]
            + c2_ref[...] * cw_ref[2:3, :] + u_ref[...] * cw_ref[3:4, :])
    c = (conv * jax.nn.sigmoid(conv)).astype(BF16)
    gact = _gate_activations(zif_ref[...], bif_ref[...])
    m0_all = m_in[...]
    rows = lax.broadcasted_iota(jnp.int32, (TB, HD), 0)
    m_new_all = jnp.zeros((TB, LANES), F32)
    lane = lax.broadcasted_iota(jnp.int32, (TB, LANES), 1)
    for h in range(ML_HEADS):
        sl = slice(h * HD, (h + 1) * HD)
        li = gact[:, h:h + 1]
        lf = gact[:, ML_HEADS + h:ML_HEADS + h + 1]
        m0 = m0_all[:, h:h + 1]
        inter = lf + m0
        m_t = jnp.maximum(inter, li)
        w_intra = jnp.exp(li - m_t)
        w_inter = jnp.exp(inter - m_t)
        qk = _dot(c[:, sl], wqk_ref[h])
        q = qk[:, 0:HD]
        k = qk[:, HD:2 * HD] * (HD ** -0.5)
        v = mv_ref[:, sl].astype(F32)
        n0 = n_in[:, sl]
        s = jnp.sum(q * k, axis=1, keepdims=True) * w_intra
        kw = k * w_intra
        vb = mv_ref[:, sl]
        qC = jnp.zeros((TB, HD), F32)
        for j in range(TB):
            Cj = C_in[j, h]
            only_j = rows == j
            qC = qC + _dot(jnp.where(only_j, q, 0.0).astype(BF16), Cj.astype(BF16))
            outer = _dot_tn(jnp.where(only_j, kw, 0.0).astype(BF16), vb)
            C_out[j, h] = w_inter[j:j + 1, :] * Cj + outer
        num = w_inter * qC + s * v
        den = w_inter * jnp.sum(q * n0, axis=1, keepdims=True) + s
        hval = num / jnp.maximum(jnp.abs(den), jnp.exp(-m_t))
        n_out[:, sl] = w_inter * n0 + kw
        m_new_all = jnp.where(lane == h, m_t, m_new_all)
        ms = jnp.mean(hval * hval, axis=-1, keepdims=True)
        y = hval * lax.rsqrt(ms + NORM_EPS) * mg_ref[...]
        hm_ref[:, sl] = (y * jax.nn.sigmoid(mo_ref[:, sl])).astype(BF16)
    m_out[...] = m_new_all


def _mlstm_sample(u, conv_state, mv, mo, zif, cw, cb, wqk, bif, mg, C0, n0, m0, TB):
    Bd, W = u.shape
    H, HD = ML_HEADS, ML_HEAD_DIM
    row = lambda i: (i, 0)
    fixed = lambda i: (0, 0)
    m0p = jnp.pad(m0, ((0, 0), (0, LANES - H)))
    outs = [
        jax.ShapeDtypeStruct((Bd, W), BF16),
        jax.ShapeDtypeStruct((Bd, H, HD, HD), F32),
        jax.ShapeDtypeStruct((Bd, W), F32),
        jax.ShapeDtypeStruct((Bd, LANES), F32),
    ]
    return pl.pallas_call(
        functools.partial(_mlstm_sample_kernel, TB=TB),
        grid=(Bd // TB,),
        in_specs=[
            pl.BlockSpec((TB, W), row),
            pl.BlockSpec((TB, W), row),
            pl.BlockSpec((TB, W), row),
            pl.BlockSpec((TB, W), row),
            pl.BlockSpec((TB, W), row),
            pl.BlockSpec((TB, W), row),
            pl.BlockSpec((TB, LANES), row),
            pl.BlockSpec((CONV_W, W), fixed),
            pl.BlockSpec((1, W), fixed),
            pl.BlockSpec((H, HD, 2 * HD), lambda i: (0, 0, 0)),
            pl.BlockSpec((1, LANES), fixed),
            pl.BlockSpec((1, HD), fixed),
            pl.BlockSpec((TB, H, HD, HD), lambda i: (i, 0, 0, 0)),
            pl.BlockSpec((TB, W), row),
            pl.BlockSpec((TB, LANES), row),
        ],
        out_specs=[
            pl.BlockSpec((TB, W), row),
            pl.BlockSpec((TB, H, HD, HD), lambda i: (i, 0, 0, 0)),
            pl.BlockSpec((TB, W), row),
            pl.BlockSpec((TB, LANES), row),
        ],
        out_shape=outs,
        compiler_params=_cparams(("parallel",)),
        name="mlstm_sample",
    )(u, conv_state[:, 0], conv_state[:, 1], conv_state[:, 2], mv, mo, zif, cw, cb, wqk, bif, mg,
      C0, n0.reshape(Bd, W), m0p)


def _merge_kernel(x_ref, oa_ref, hm_ref, g1_ref, wg_ref, bm_ref, wa_ref, wb_ref, wo_ref, o_ref):
    x = x_ref[...]
    D = x.shape[1]
    ms = jnp.mean(x * x, axis=-1, keepdims=True)
    h = (x * lax.rsqrt(ms + NORM_EPS) * g1_ref[...]).astype(BF16)
    g = jax.nn.sigmoid(_dot(h, wg_ref[...]) + bm_ref[...])
    ya = _dot(oa_ref[...], wa_ref[...])
    yb = _dot(hm_ref[...], wb_ref[...])
    mix = (g[:, 0:D] * ya + g[:, D:2 * D] * yb).astype(BF16)
    o_ref[...] = x + _dot(mix, wo_ref[...])


def _merge(x2, oa, hm, g1, wg, bm, wa, wb, wo, tm):
    M, D = x2.shape
    W = oa.shape[1]
    row = lambda i: (i, 0)
    fixed = lambda i: (0, 0)
    return pl.pallas_call(
        _merge_kernel,
        grid=(M // tm,),
        in_specs=[
            pl.BlockSpec((tm, D), row),
            pl.BlockSpec((tm, W), row),
            pl.BlockSpec((tm, W), row),
            pl.BlockSpec((1, D), fixed),
            pl.BlockSpec(wg.shape, fixed),
            pl.BlockSpec((1, 2 * D), fixed),
            pl.BlockSpec(wa.shape, fixed),
            pl.BlockSpec(wb.shape, fixed),
            pl.BlockSpec(wo.shape, fixed),
        ],
        out_specs=pl.BlockSpec((tm, D), row),
        out_shape=jax.ShapeDtypeStruct((M, D), F32),
        compiler_params=_cparams(("parallel",)),
        name="merge",
    )(x2, oa, hm, g1, wg, bm, wa, wb, wo)


def _ffn_kernel(x_ref, g2_ref, wgate_ref, wup_ref, wdown_ref, o_ref):
    x = x_ref[...]
    ms = jnp.mean(x * x, axis=-1, keepdims=True)
    h = (x * lax.rsqrt(ms + NORM_EPS) * g2_ref[...]).astype(BF16)
    a = _dot(h, wgate_ref[...])
    b = _dot(h, wup_ref[...])
    act = (a * jax.nn.sigmoid(a) * b).astype(BF16)
    o_ref[...] = x + _dot(act, wdown_ref[...])


def _ffn(x2, g2, wgate, wup, wdown, tm):
    M, D = x2.shape
    row = lambda i: (i, 0)
    fixed = lambda i: (0, 0)
    return pl.pallas_call(
        _ffn_kernel,
        grid=(M // tm,),
        in_specs=[
            pl.BlockSpec((tm, D), row),
            pl.BlockSpec((1, D), fixed),
            pl.BlockSpec(wgate.shape, fixed),
            pl.BlockSpec(wup.shape, fixed),
            pl.BlockSpec(wdown.shape, fixed),
        ],
        out_specs=pl.BlockSpec((tm, D), row),
        out_shape=jax.ShapeDtypeStruct((M, D), F32),
        compiler_params=_cparams(("parallel",)),
        name="ffn",
    )(x2, g2, wgate, wup, wdown)


def _rope_tables(pos):
    half = DA_HEAD_DIM // 2
    inv = ROPE_THETA ** (-jnp.arange(half, dtype=F32) / half)
    ang = pos.astype(F32)[:, None] * inv[None, :]
    cos = jnp.cos(ang)
    sin = jnp.sin(ang)
    cos = jnp.concatenate([cos, cos], axis=1)
    sin = jnp.concatenate([-sin, sin], axis=1)
    reps = LANES // DA_HEAD_DIM
    return jnp.concatenate([cos] * reps, axis=1), jnp.concatenate([sin] * reps, axis=1)


def _pick_tile(M, pref):
    t = min(pref, M)
    while M % t:
        t //= 2
    return t


def _layer_weights(w_in, qnorm_g, knorm_g, lambda_q1, lambda_k1, lambda_q2, lambda_k2, subln_g, w_a_out,
                   conv_w, conv_b, w_mq, w_mk, b_igate, b_fgate, mnorm_g, w_b_out, b_merge, w_o,
                   w_ffn_gate, w_ffn_up, w_ffn_down):
    W = DA_HEADS * DA_V_DIM
    n_main = 6 * W
    n_if = 2 * ML_HEADS
    w_if = jnp.pad(w_in[:, n_main:n_main + n_if], ((0, 0), (0, LANES - n_if)))
    p = {}
    p['w_main'] = jnp.concatenate([w_in[:, :n_main], w_if], axis=1).astype(BF16)
    p['w_gates'] = w_in[:, n_main + n_if:].astype(BF16)
    p['qg'] = jnp.tile(qnorm_g, 2 * DA_HEADS)[None, :]
    p['kg'] = jnp.tile(knorm_g, 2 * DA_HEADS)[None, :]
    grp = np.arange(W) // DA_HEAD_DIM
    p['gm'] = jnp.asarray((grp[:, None] == grp[None, :]).astype(np.float32) / DA_HEAD_DIM, dtype=BF16)
    p['lams'] = jnp.stack([lambda_q1, lambda_k1, lambda_q2, lambda_k2]).astype(F32)
    p['sg'] = subln_g[None, :]
    p['w_a_out'] = w_a_out.astype(BF16)
    p['conv_w'] = conv_w
    p['conv_b'] = conv_b[None, :]
    p['wqk'] = jnp.concatenate([w_mq, w_mk], axis=2).astype(BF16)
    p['bif'] = jnp.pad(jnp.concatenate([b_igate, b_fgate]), (0, LANES - n_if))[None, :]
    p['mg'] = mnorm_g[None, :]
    p['w_b_out'] = w_b_out.astype(BF16)
    p['b_merge'] = b_merge[None, :]
    p['w_o'] = w_o.astype(BF16)
    p['w_ffn_gate'] = w_ffn_gate.astype(BF16)
    p['w_ffn_up'] = w_ffn_up.astype(BF16)
    p['w_ffn_down'] = w_ffn_down.astype(BF16)
    return p


def kernel(x_prompt, x_sample, cache_k, cache_v, page_table, state_C, state_n, state_m, state_conv, norm1_g, w_in, qnorm_g, knorm_g, lambda_q1, lambda_k1, lambda_q2, lambda_k2, subln_g, w_a_out, conv_w, conv_b, w_mq, w_mk, b_igate, b_fgate, mnorm_g, w_b_out, b_merge, w_o, norm2_g, w_ffn_gate, w_ffn_up, w_ffn_down):
    Bp, Tp, D = x_prompt.shape
    Bs, Ts, _ = x_sample.shape
    assert Ts == 1, "the sample group decodes one token per sequence"
    depth = w_in.shape[0]
    n_pages = page_table.shape[1]
    page_rows = cache_k.shape[2]
    past_len = n_pages * page_rows
    W = DA_HEADS * DA_V_DIM
    H, HD = ML_HEADS, ML_HEAD_DIM

    Mp = Bp * Tp
    tm_p = _pick_tile(Tp, 512)
    tm_merge = _pick_tile(Mp, 1024)
    tm_s = _pick_tile(Bs, 128)
    cos_p, sin_p = _rope_tables(jnp.arange(Tp))
    cos_s, sin_s = _rope_tables(past_len + jnp.arange(Ts))
    cos_s = jnp.tile(cos_s, (tm_s, 1))
    sin_s = jnp.tile(sin_s, (tm_s, 1))
    tq = _pick_tile(Tp, 2048)
    rq_full = min(tq, 128)
    rq_diag = min(tq, 512)
    L = _pick_tile(Tp, 256)
    G = _pick_tile(n_pages, 32)
    TB = _pick_tile(Bs, 8)

    yp = x_prompt.reshape(Mp, D)
    ys = x_sample.reshape(Bs, D)
    outs = {k: [] for k in ('kp', 'vp', 'Cp', 'np', 'mp', 'cp', 'ks', 'vs', 'Cs', 'ns', 'ms', 'cs')}
    for l in range(depth):
        p = _layer_weights(w_in[l], qnorm_g[l], knorm_g[l], lambda_q1[l], lambda_k1[l], lambda_q2[l],
                           lambda_k2[l], subln_g[l], w_a_out[l], conv_w[l], conv_b[l], w_mq[l], w_mk[l],
                           b_igate[l], b_fgate[l], mnorm_g[l], w_b_out[l], b_merge[l], w_o[l],
                           w_ffn_gate[l], w_ffn_up[l], w_ffn_down[l])
        g1 = norm1_g[l][None, :]
        g2 = norm2_g[l][None, :]
        lam_init = 0.8 - 0.6 * math.exp(-0.3 * l)

        q, k, kb, v, vb, u, mv, mo, zif = _in_proj(yp, g1, p['w_main'], p['qg'], p['kg'], cos_p, sin_p,
                                                   p['gm'], tm_p, BF16, True)
        oa = _attn_prompt(q.reshape(Bp, Tp, W), kb.reshape(Bp, Tp, W), vb.reshape(Bp, Tp, W),
                          p['lams'], p['sg'], lam_init, tq, rq_full, rq_diag)
        hm, Cp, np_, mp = _mlstm_prompt(u.reshape(Bp, Tp, W), mv.reshape(Bp, Tp, W), mo.reshape(Bp, Tp, W),
                                        zif.reshape(Bp, Tp, LANES), p['conv_w'], p['conv_b'], p['wqk'],
                                        p['bif'], p['mg'], L)
        x1 = _merge(yp, oa.reshape(Mp, W), hm.reshape(Mp, W), g1, p['w_gates'], p['b_merge'],
                    p['w_a_out'], p['w_b_out'], p['w_o'], tm_merge)
        yp = _ffn(x1, g2, p['w_ffn_gate'], p['w_ffn_up'], p['w_ffn_down'], tm_p)
        outs['kp'].append(jnp.transpose(k.reshape(Bp, 2 * DA_HEADS, DA_HEAD_DIM, Tp), (0, 3, 1, 2)))
        outs['vp'].append(v.reshape(Bp, Tp, DA_HEADS, DA_V_DIM))
        outs['Cp'].append(Cp)
        outs['np'].append(np_[:, :H, :])
        outs['mp'].append(mp[:, :H, 0])
        outs['cp'].append(u.reshape(Bp, Tp, W)[:, Tp - (CONV_W - 1):, :])

        q, k, kb, v, vb, u, mv, mo, zif = _in_proj(ys, g1, p['w_main'], p['qg'], p['kg'], cos_s, sin_s,
                                                   p['gm'], tm_s, F32, False)
        n_phys = cache_k.shape[1]
        ckt = jnp.transpose(cache_k[l], (0, 2, 3, 1)).reshape(n_phys, W, page_rows)
        cv2 = cache_v[l].reshape(n_phys, page_rows * DA_HEADS, DA_V_DIM)
        oa = _attn_sample(q, k, v, ckt, cv2, page_table, p['lams'], p['sg'], lam_init, G)
        hm, Cs, ns, ms = _mlstm_sample(u, state_conv[l], mv, mo, zif, p['conv_w'], p['conv_b'], p['wqk'],
                                       p['bif'], p['mg'], state_C[l], state_n[l], state_m[l], TB)
        x1 = _merge(ys, oa, hm, g1, p['w_gates'], p['b_merge'], p['w_a_out'], p['w_b_out'], p['w_o'], tm_s)
        ys = _ffn(x1, g2, p['w_ffn_gate'], p['w_ffn_up'], p['w_ffn_down'], tm_s)
        outs['ks'].append(k.reshape(Bs, Ts, 2 * DA_HEADS, DA_HEAD_DIM))
        outs['vs'].append(v.reshape(Bs, Ts, DA_HEADS, DA_V_DIM))
        outs['Cs'].append(Cs)
        outs['ns'].append(ns.reshape(Bs, H, HD))
        outs['ms'].append(ms[:, :H])
        outs['cs'].append(jnp.concatenate([state_conv[l][:, 1:, :], u[:, None, :]], axis=1))

    st = lambda name: jnp.stack(outs[name])
    return (yp.reshape(Bp, Tp, D), ys.reshape(Bs, Ts, D),
            st('kp'), st('vp'), st('Cp'), st('np'), st('mp'), st('cp'),
            st('ks'), st('vs'), st('Cs'), st('ns'), st('ms'), st('cs'))
```
